```python
import jax, jax.numpy as jnp
from jax import lax
import numpy as np

D_MODEL = 1024
BATCH = 2
SEQ = 8192
DEPTH = 2
DEC_BATCH = 16
DEC_SEQ = 32
PAST_LEN = 2048

CHUNK = 64
N_MIXERS = 2
D_FF = 2816
D_A = 2 * D_MODEL
CHUNK_A = 128
N_GROUPS_A = 8
D_GROUP_A = D_A // N_GROUPS_A
D_CONV = D_MODEL
CONV_W = 31
N_A_LAYERS = (DEPTH + 1) // 2
N_B_LAYERS = DEPTH // 2
EPS = 1e-6

kernel_name = "streaming_gmlp_conformer_hybrid"


def _rms_norm(x, g):
    xf = x.astype(jnp.float32)
    y = xf * lax.rsqrt(jnp.mean(xf * xf, axis=-1, keepdims=True) + EPS)
    return (y * g.astype(jnp.float32)).astype(x.dtype)


def _layer_norm(x, g, b):
    xf = x.astype(jnp.float32)
    mu = jnp.mean(xf, axis=-1, keepdims=True)
    xc = xf - mu
    y = xc * lax.rsqrt(jnp.mean(xc * xc, axis=-1, keepdims=True) + EPS)
    return (y * g.astype(jnp.float32) + b.astype(jnp.float32)).astype(x.dtype)


def _swiglu_half(x, g, w_in, w_out):
    h = _rms_norm(x, g)
    a, b = jnp.split(h @ w_in, 2, axis=-1)
    return x + 0.5 * ((jax.nn.silu(a) * b) @ w_out)


def _gmlp_mixer(h, w_in, v_g, v_b, w_s, b_s, w_out):
    bsz, t, _ = h.shape
    u, v = jnp.split(jax.nn.gelu(h @ w_in, approximate=False), 2, axis=-1)
    v = _layer_norm(v, v_g, v_b)
    L = min(t, CHUNK_A)
    n = t // L
    mask = jnp.tril(jnp.ones((L, L), dtype=bool))
    ws = jnp.where(mask[None], w_s[:, :L, :L], 0).astype(h.dtype)
    vc = v.reshape(bsz, n, L, N_GROUPS_A, D_GROUP_A)
    z = jnp.einsum('gts,bnsgc->bntgc', ws, vc) + b_s[:, :L].T[None, None, :, :, None]
    y = u * z.reshape(bsz, t, D_A)
    return y @ w_out, v[:, t - L:]


def _conv_module(h, past, w_in, dw_w, dw_b, ln_g, ln_b, w_out):
    a, gate = jnp.split(h @ w_in, 2, axis=-1)
    glu = a * jax.nn.sigmoid(gate)
    xp = jnp.concatenate([past.astype(glu.dtype), glu], axis=1)
    new_past = xp[:, xp.shape[1] - (CONV_W - 1):]
    c = lax.conv_general_dilated(
        xp, dw_w[:, None, :].astype(xp.dtype), window_strides=(1,), padding='VALID',
        dimension_numbers=('NWC', 'WIO', 'NWC'), feature_group_count=D_CONV) + dw_b
    c = jax.nn.silu(_layer_norm(c, ln_g, ln_b))
    return c @ w_out, new_past


def _trunk(x, conv_past, norm_g, ffn_w_in, ffn_w_out, a_w_in, a_v_ln_g, a_v_ln_b, a_w_s,
           a_b_s, a_w_out, b_w_in, b_dw_w, b_dw_b, b_ln_g, b_ln_b, b_w_out, final_norm_g):
    a_states, conv_states = [], []
    for i in range(DEPTH):
        x = _swiglu_half(x, norm_g[i, 0], ffn_w_in[i, 0], ffn_w_out[i, 0])
        h = _rms_norm(x, norm_g[i, 1])
        j = i // N_MIXERS
        if i % N_MIXERS == 0:
            out, st = _gmlp_mixer(h, a_w_in[j], a_v_ln_g[j], a_v_ln_b[j], a_w_s[j], a_b_s[j], a_w_out[j])
            a_states.append(st)
        else:
            out, st = _conv_module(h, conv_past[j], b_w_in[j], b_dw_w[j], b_dw_b[j],
                                   b_ln_g[j], b_ln_b[j], b_w_out[j])
            conv_states.append(st)
        x = x + out
        x = _swiglu_half(x, norm_g[i, 2], ffn_w_in[i, 1], ffn_w_out[i, 1])
    y = _rms_norm(x, final_norm_g)
    return y, jnp.stack(a_states), jnp.stack(conv_states)


def setup_inputs(seed: int = 0) -> dict:
    key = jax.random.key(seed)
    ks = jax.random.split(key, 20)
    f32 = jnp.float32
    nrm = lambda k, shape, s: jax.random.normal(k, shape, f32) * s
    return {
        "x_prompt": nrm(ks[0], (BATCH, SEQ, D_MODEL), 1.0),
        "x_sample": nrm(ks[1], (DEC_BATCH, DEC_SEQ, D_MODEL), 1.0),
        "cache_conv": nrm(ks[2], (N_B_LAYERS, DEC_BATCH, CONV_W - 1, D_CONV), 0.5),
        "norm_g": 1.0 + nrm(ks[3], (DEPTH, 3, D_MODEL), 0.02),
        "ffn_w_in": nrm(ks[4], (DEPTH, 2, D_MODEL, 2 * D_FF), D_MODEL ** -0.5),
        "ffn_w_out": nrm(ks[5], (DEPTH, 2, D_FF, D_MODEL), D_FF ** -0.5),
        "a_w_in": nrm(ks[6], (N_A_LAYERS, D_MODEL, 2 * D_A), D_MODEL ** -0.5),
        "a_v_ln_g": 1.0 + nrm(ks[7], (N_A_LAYERS, D_A), 0.02),
        "a_v_ln_b": nrm(ks[8], (N_A_LAYERS, D_A), 0.02),
        "a_w_s": nrm(ks[9], (N_A_LAYERS, N_GROUPS_A, CHUNK_A, CHUNK_A), CHUNK_A ** -0.5),
        "a_b_s": 1.0 + nrm(ks[10], (N_A_LAYERS, N_GROUPS_A, CHUNK_A), 0.02),
        "a_w_out": nrm(ks[11], (N_A_LAYERS, D_A, D_MODEL), D_A ** -0.5),
        "b_w_in": nrm(ks[12], (N_B_LAYERS, D_MODEL, 2 * D_CONV), D_MODEL ** -0.5),
        "b_dw_w": nrm(ks[13], (N_B_LAYERS, CONV_W, D_CONV), CONV_W ** -0.5),
        "b_dw_b": nrm(ks[14], (N_B_LAYERS, D_CONV), 0.01),
        "b_ln_g": 1.0 + nrm(ks[15], (N_B_LAYERS, D_CONV), 0.02),
        "b_ln_b": nrm(ks[16], (N_B_LAYERS, D_CONV), 0.02),
        "b_w_out": nrm(ks[17], (N_B_LAYERS, D_CONV, D_MODEL), D_CONV ** -0.5),
        "final_norm_g": 1.0 + nrm(ks[18], (D_MODEL,), 0.02),
    }


def reference(x_prompt, x_sample, cache_conv, norm_g, ffn_w_in, ffn_w_out, a_w_in, a_v_ln_g,
              a_v_ln_b, a_w_s, a_b_s, a_w_out, b_w_in, b_dw_w, b_dw_b, b_ln_g, b_ln_b, b_w_out,
              final_norm_g):
    params = (norm_g, ffn_w_in, ffn_w_out, a_w_in, a_v_ln_g, a_v_ln_b, a_w_s, a_b_s, a_w_out,
              b_w_in, b_dw_w, b_dw_b, b_ln_g, b_ln_b, b_w_out, final_norm_g)
    prompt_past = jnp.zeros((N_B_LAYERS, x_prompt.shape[0], CONV_W - 1, D_CONV), x_prompt.dtype)
    y_prompt, state_gmlp_v_prompt, state_conv_prompt = _trunk(x_prompt, prompt_past, *params)
    y_sample, state_gmlp_v_sample, state_conv_sample = _trunk(x_sample, cache_conv, *params)
    return (y_prompt, y_sample, state_gmlp_v_prompt, state_gmlp_v_sample, state_conv_prompt, state_conv_sample)
```

```python
import functools

import jax
import jax.numpy as jnp
from jax import lax
from jax.experimental import pallas as pl
from jax.experimental.pallas import tpu as pltpu

D_MODEL = 1024
D_FF = 2816
D_A = 2 * D_MODEL
CHUNK_A = 128
N_GROUPS_A = 8
D_GROUP_A = D_A // N_GROUPS_A
D_CONV = D_MODEL
CONV_W = 31
EPS = 1e-6

TM = 512
RB = 32
CW = 256
HIST = 32
CONV_OFF = HIST - (CONV_W - 1)
VMEM_LIMIT = 56 * 1024 * 1024

_F32 = jnp.float32
_BF16 = jnp.bfloat16


def _const_spec(shape):
    nd = len(shape)
    return pl.BlockSpec(shape, lambda i: (0,) * nd, pipeline_mode=pl.Buffered(1))


def _rms_to_bf16(x_ref, g_ref, h_ref):
    def body(r, carry):
        rows = pl.ds(pl.multiple_of(r * RB, RB), RB)
        x = x_ref[rows, :]
        ms = jnp.mean(x * x, axis=-1, keepdims=True)
        h_ref[rows, :] = (x * lax.rsqrt(ms + EPS) * g_ref[...]).astype(_BF16)
        return carry
    lax.fori_loop(0, TM // RB, body, 0)


def _ffn_kernel(x_ref, g_ref, win_ref, wout_ref, *rest, final_norm):
    if final_norm:
        gf_ref, o_ref, h_ref, act_ref = rest
    else:
        o_ref, h_ref, act_ref = rest
    _rms_to_bf16(x_ref, g_ref, h_ref)
    for c in range(D_FF // CW):
        ab = jnp.dot(h_ref[...], win_ref[c], preferred_element_type=_F32)
        a = ab[:, :CW]
        b = ab[:, CW:]
        act_ref[:, c * CW:(c + 1) * CW] = (a * jax.nn.sigmoid(a) * b).astype(_BF16)
    y = jnp.dot(act_ref[...], wout_ref[...], preferred_element_type=_F32)
    o_ref[...] = x_ref[...] + 0.5 * y
    if final_norm:
        def body(r, carry):
            rows = pl.ds(pl.multiple_of(r * RB, RB), RB)
            x = o_ref[rows, :]
            ms = jnp.mean(x * x, axis=-1, keepdims=True)
            o_ref[rows, :] = x * lax.rsqrt(ms + EPS) * gf_ref[...]
            return carry
        lax.fori_loop(0, TM // RB, body, 0)


def _ffn_call(x, g, win, wout, gf=None):
    n_tok = x.shape[0]
    final_norm = gf is not None
    tile = pl.BlockSpec((TM, D_MODEL), lambda i: (i, 0))
    in_specs = [tile, _const_spec((1, D_MODEL)), _const_spec(win.shape), _const_spec(wout.shape)]
    args = [x, g, win, wout]
    if final_norm:
        in_specs.append(_const_spec((1, D_MODEL)))
        args.append(gf)
    return pl.pallas_call(
        functools.partial(_ffn_kernel, final_norm=final_norm),
        grid=(n_tok // TM,),
        in_specs=in_specs,
        out_specs=tile,
        out_shape=jax.ShapeDtypeStruct((n_tok, D_MODEL), _F32),
        scratch_shapes=[pltpu.VMEM((TM, D_MODEL), _BF16), pltpu.VMEM((TM, D_FF), _BF16)],
        compiler_params=pltpu.CompilerParams(
            dimension_semantics=("arbitrary",), vmem_limit_bytes=VMEM_LIMIT),
        name="ffn_final" if final_norm else "ffn",
    )(*args)


def _gmlp_kernel(x_ref, g_ref, win_ref, vg_ref, vb_ref, ws_ref, bs_ref, wout_ref,
                 o_ref, vst_ref, h_ref, uv_ref, vnb_ref, wsb_ref, y_ref, *, n_prompt_tiles):
    i = pl.program_id(0)
    _rms_to_bf16(x_ref, g_ref, h_ref)

    ncol = 2 * CW
    for c in range(2 * D_A // ncol):
        t = jnp.dot(h_ref[...], win_ref[:, c * ncol:(c + 1) * ncol], preferred_element_type=_F32)
        uv_ref[:, c * ncol:(c + 1) * ncol] = 0.5 * t * (1.0 + lax.erf(t * (2.0 ** -0.5)))

    lrb = 16
    def ln_body(r, carry):
        rows = pl.ds(pl.multiple_of(r * lrb, lrb), lrb)
        v = uv_ref[rows, D_A:]
        mu = jnp.mean(v, axis=-1, keepdims=True)
        xc = v - mu
        var = jnp.mean(xc * xc, axis=-1, keepdims=True)
        vn = xc * lax.rsqrt(var + EPS) * vg_ref[...] + vb_ref[...]
        vst_ref[0, rows, :] = vn
        vnb_ref[rows, :] = vn.astype(_BF16)
        return carry
    lax.fori_loop(0, TM // lrb, ln_body, 0)

    row = lax.broadcasted_iota(jnp.int32, (CHUNK_A, CHUNK_A), 0)
    col = lax.broadcasted_iota(jnp.int32, (CHUNK_A, CHUNK_A), 1)
    same_stream = jnp.logical_or(i < n_prompt_tiles, (row // 32) == (col // 32))
    keep = jnp.logical_and(col <= row, same_stream)
    for gi in range(N_GROUPS_A):
        wsb_ref[gi] = jnp.where(keep, ws_ref[0, gi], 0.0).astype(_BF16)

    for n in range(TM // CHUNK_A):
        rows = slice(n * CHUNK_A, (n + 1) * CHUNK_A)
        for gi in range(N_GROUPS_A):
            cols = slice(gi * D_GROUP_A, (gi + 1) * D_GROUP_A)
            z = jnp.dot(wsb_ref[gi], vnb_ref[rows, cols], preferred_element_type=_F32)
            z = z + bs_ref[0, gi]
            y_ref[rows, cols] = (uv_ref[rows, cols] * z).astype(_BF16)

    out = jnp.dot(y_ref[...], wout_ref[...], preferred_element_type=_F32)
    o_ref[...] = x_ref[...] + out


def _gmlp_call(x, g, win, vg, vb, ws2, bs2, wout, n_prompt_tiles, tiles_per_stream):
    n_tok = x.shape[0]
    n_tiles = n_tok // TM
    n_states = (n_tiles + tiles_per_stream - 1) // tiles_per_stream
    tile = pl.BlockSpec((TM, D_MODEL), lambda i: (i, 0))
    return pl.pallas_call(
        functools.partial(_gmlp_kernel, n_prompt_tiles=n_prompt_tiles),
        grid=(n_tiles,),
        in_specs=[
            tile, _const_spec((1, D_MODEL)), _const_spec(win.shape),
            _const_spec((1, D_A)), _const_spec((1, D_A)),
            pl.BlockSpec((1, N_GROUPS_A, CHUNK_A, CHUNK_A), lambda i: (i // n_prompt_tiles, 0, 0, 0)),
            pl.BlockSpec((1, N_GROUPS_A, CHUNK_A, D_GROUP_A), lambda i: (i // n_prompt_tiles, 0, 0, 0)),
            _const_spec(wout.shape),
        ],
        out_specs=[tile, pl.BlockSpec((1, TM, D_A), lambda i: (i // tiles_per_stream, 0, 0))],
        out_shape=[jax.ShapeDtypeStruct((n_tok, D_MODEL), _F32),
                   jax.ShapeDtypeStruct((n_states, TM, D_A), _F32)],
        scratch_shapes=[
            pltpu.VMEM((TM, D_MODEL), _BF16),
            pltpu.VMEM((TM, 2 * D_A), _F32),
            pltpu.VMEM((TM, D_A), _BF16),
            pltpu.VMEM((N_GROUPS_A, CHUNK_A, CHUNK_A), _BF16),
            pltpu.VMEM((TM, D_A), _BF16),
        ],
        compiler_params=pltpu.CompilerParams(
            dimension_semantics=("arbitrary",), vmem_limit_bytes=VMEM_LIMIT),
        name="gmlp",
    )(x, g, win, vg, vb, ws2, bs2, wout)


def _conv_kernel(x_ref, g_ref, win_ref, cache_ref, dww_ref, dwb_ref, lng_ref, lnb_ref, wout_ref,
                 o_ref, glu_ref, h_ref, xp_ref, c_ref, cn_ref,
                 *, n_prompt_tiles, tiles_per_stream, n_dec, dec_seq):
    i = pl.program_id(0)
    _rms_to_bf16(x_ref, g_ref, h_ref)

    for c in range(D_CONV // CW):
        ag = jnp.dot(h_ref[...], win_ref[c], preferred_element_type=_F32)
        glu_ref[0, :, c * CW:(c + 1) * CW] = ag[:, :CW] * jax.nn.sigmoid(ag[:, CW:])

    def conv_block(src, dst):
        for cb in range(D_CONV // 128):
            cols = slice(cb * 128, (cb + 1) * 128)
            win = xp_ref[pl.ds(src, RB + HIST), cols]
            acc = jnp.zeros((RB, 128), _F32)
            for k in range(CONV_W):
                acc = acc + win[CONV_OFF + k:CONV_OFF + k + RB, :] * dww_ref[k:k + 1, cols]
            c_ref[pl.ds(dst, RB), cols] = acc + dwb_ref[:, cols]

    @pl.when(i < n_prompt_tiles)
    def _prompt():
        @pl.when(i % tiles_per_stream == 0)
        def _new_stream():
            xp_ref[0:HIST, :] = jnp.zeros((HIST, D_CONV), _F32)
        xp_ref[HIST:HIST + TM, :] = glu_ref[0]

        def body(r, carry):
            base = pl.multiple_of(r * RB, RB)
            conv_block(base, base)
            return carry
        lax.fori_loop(0, TM // RB, body, 0)
        xp_ref[0:HIST, :] = xp_ref[TM:TM + HIST, :]

    @pl.when(i >= n_prompt_tiles)
    def _sample():
        seg = HIST + dec_seq
        for s in range(n_dec):
            xp_ref[s * seg:s * seg + HIST, :] = cache_ref[s]
            xp_ref[s * seg + HIST:(s + 1) * seg, :] = glu_ref[0, s * dec_seq:(s + 1) * dec_seq, :]

        def body(s, carry):
            conv_block(pl.multiple_of(s * seg, seg), pl.multiple_of(s * dec_seq, dec_seq))
            return carry
        lax.fori_loop(0, n_dec, body, 0)

    def ln_body(r, carry):
        rows = pl.ds(pl.multiple_of(r * RB, RB), RB)
        c = c_ref[rows, :]
        mu = jnp.mean(c, axis=-1, keepdims=True)
        xc = c - mu
        var = jnp.mean(xc * xc, axis=-1, keepdims=True)
        cn = xc * lax.rsqrt(var + EPS) * lng_ref[...] + lnb_ref[...]
        cn_ref[rows, :] = (cn * jax.nn.sigmoid(cn)).astype(_BF16)
        return carry
    lax.fori_loop(0, TM // RB, ln_body, 0)

    out = jnp.dot(cn_ref[...], wout_ref[...], preferred_element_type=_F32)
    o_ref[...] = x_ref[...] + out


def _conv_call(x, g, win, cache_pad, dww, dwb, lng, lnb, wout, n_prompt_tiles, tiles_per_stream):
    n_tok = x.shape[0]
    n_tiles = n_tok // TM
    n_states = (n_tiles + tiles_per_stream - 1) // tiles_per_stream
    n_dec, hist, _ = cache_pad.shape
    dec_seq = TM // n_dec
    assert hist == HIST and dec_seq == RB and n_tiles == n_prompt_tiles + 1
    tile = pl.BlockSpec((TM, D_MODEL), lambda i: (i, 0))
    xp_rows = max(TM + HIST, n_dec * (HIST + dec_seq))
    return pl.pallas_call(
        functools.partial(_conv_kernel, n_prompt_tiles=n_prompt_tiles,
                          tiles_per_stream=tiles_per_stream, n_dec=n_dec, dec_seq=dec_seq),
        grid=(n_tiles,),
        in_specs=[
            tile, _const_spec((1, D_MODEL)), _const_spec(win.shape), _const_spec(cache_pad.shape),
            _const_spec(dww.shape), _const_spec((1, D_CONV)), _const_spec((1, D_CONV)),
            _const_spec((1, D_CONV)), _const_spec(wout.shape),
        ],
        out_specs=[tile, pl.BlockSpec((1, TM, D_CONV), lambda i: (i // tiles_per_stream, 0, 0))],
        out_shape=[jax.ShapeDtypeStruct((n_tok, D_MODEL), _F32),
                   jax.ShapeDtypeStruct((n_states, TM, D_CONV), _F32)],
        scratch_shapes=[
            pltpu.VMEM((TM, D_MODEL), _BF16),
            pltpu.VMEM((xp_rows, D_CONV), _F32),
            pltpu.VMEM((TM, D_CONV), _F32),
            pltpu.VMEM((TM, D_CONV), _BF16),
        ],
        compiler_params=pltpu.CompilerParams(
            dimension_semantics=("arbitrary",), vmem_limit_bytes=VMEM_LIMIT),
        name="convmod",
    )(x, g, win, cache_pad, dww, dwb, lng, lnb, wout)


def _pair_chunks(w, half):
    k = w.shape[0]
    w = w.reshape(k, 2, half // CW, CW).transpose(2, 0, 1, 3).reshape(half // CW, k, 2 * CW)
    return w.astype(_BF16)


def kernel(x_prompt, x_sample, cache_conv, norm_g, ffn_w_in, ffn_w_out, a_w_in, a_v_ln_g, a_v_ln_b,
           a_w_s, a_b_s, a_w_out, b_w_in, b_dw_w, b_dw_b, b_ln_g, b_ln_b, b_w_out, final_norm_g):
    batch, seq, _ = x_prompt.shape
    n_dec, dec_seq, _ = x_sample.shape
    depth = norm_g.shape[0]
    assert seq % TM == 0 and n_dec * dec_seq == TM and CHUNK_A % dec_seq == 0
    tiles_per_stream = seq // TM
    n_prompt_tiles = batch * tiles_per_stream

    x = jnp.concatenate([x_prompt.reshape(batch * seq, D_MODEL),
                         x_sample.reshape(n_dec * dec_seq, D_MODEL)], axis=0)

    a_states, conv_states = [], []
    for i in range(depth):
        j = i // 2
        x = _ffn_call(x, norm_g[i, 0][None], _pair_chunks(ffn_w_in[i, 0], D_FF),
                      ffn_w_out[i, 0].astype(_BF16))
        if i % 2 == 0:
            rep = CHUNK_A // dec_seq
            ws2 = jnp.stack([a_w_s[j], jnp.tile(a_w_s[j][:, :dec_seq, :dec_seq], (1, rep, rep))])
            bs_p = jnp.broadcast_to(a_b_s[j][:, :, None], (N_GROUPS_A, CHUNK_A, D_GROUP_A))
            bs_s = jnp.broadcast_to(jnp.tile(a_b_s[j][:, :dec_seq], (1, rep))[:, :, None],
                                    (N_GROUPS_A, CHUNK_A, D_GROUP_A))
            x, vst = _gmlp_call(x, norm_g[i, 1][None], a_w_in[j].astype(_BF16),
                                a_v_ln_g[j][None], a_v_ln_b[j][None], ws2, jnp.stack([bs_p, bs_s]),
                                a_w_out[j].astype(_BF16), n_prompt_tiles, tiles_per_stream)
            a_states.append(vst)
        else:
            cache_pad = jnp.pad(cache_conv[j], ((0, 0), (HIST - (CONV_W - 1), 0), (0, 0)))
            dww = jnp.pad(b_dw_w[j], ((0, 32 - CONV_W), (0, 0)))
            x, gst = _conv_call(x, norm_g[i, 1][None], _pair_chunks(b_w_in[j], D_CONV), cache_pad,
                                dww, b_dw_b[j][None], b_ln_g[j][None], b_ln_b[j][None],
                                b_w_out[j].astype(_BF16), n_prompt_tiles, tiles_per_stream)
            conv_states.append(gst)
        last = i == depth - 1
        x = _ffn_call(x, norm_g[i, 2][None], _pair_chunks(ffn_w_in[i, 1], D_FF),
                      ffn_w_out[i, 1].astype(_BF16), final_norm_g[None] if last else None)

    n_p = batch * seq
    y_prompt = x[:n_p].reshape(batch, seq, D_MODEL)
    y_sample = x[n_p:].reshape(n_dec, dec_seq, D_MODEL)
    vst = jnp.stack(a_states)
    gst = jnp.stack(conv_states)
    v_prompt = vst[:, :batch, TM - CHUNK_A:, :]
    v_sample = vst[:, batch].reshape(-1, n_dec, dec_seq, D_A)
    g_prompt = gst[:, :batch, TM - (CONV_W - 1):, :]
    g_sample = gst[:, batch].reshape(-1, n_dec, dec_seq, D_CONV)[:, :, dec_seq - (CONV_W - 1):, :]
    return (y_prompt, y_sample, v_prompt, v_sample, g_prompt, g_sample)
```

```python
import functools

import jax
import jax.numpy as jnp
from jax import lax
from jax.experimental import pallas as pl
from jax.experimental.pallas import tpu as pltpu

D_MODEL = 1024
D_FF = 2816
D_A = 2 * D_MODEL
CHUNK_A = 128
N_GROUPS_A = 8
D_GROUP_A = D_A // N_GROUPS_A
D_CONV = D_MODEL
CONV_W = 31
EPS = 1e-6

TM = 512
RB = 32
CONV_RB = 64
CW = 256
SUBLANES = 8
HIST = 32
CONV_OFF = HIST - (CONV_W - 1)
VMEM_LIMIT = 56 * 1024 * 1024

_F32 = jnp.float32
_BF16 = jnp.bfloat16


def _const_spec(shape):
    nd = len(shape)
    return pl.BlockSpec(shape, lambda i: (0,) * nd, pipeline_mode=pl.Buffered(1))


def _row_loop(n_rows, rb, unroll, body):
    def step(r, carry):
        body(pl.ds(pl.multiple_of(r * rb, rb), rb))
        return carry
    lax.fori_loop(0, n_rows // rb, step, 0, unroll=unroll)


def _rms_rows(x, g_ref):
    ms = jnp.mean(x * x, axis=-1, keepdims=True)
    return x * lax.rsqrt(ms + EPS) * g_ref[...]


def _rms_to_bf16(x_ref, g_ref, h_ref, copy_ref=None):
    def body(rows):
        x = x_ref[rows, :]
        if copy_ref is not None:
            copy_ref[rows, :] = x
        h_ref[rows, :] = _rms_rows(x, g_ref).astype(_BF16)
    _row_loop(TM, RB, 4, body)


def _tile_map(n_w, n_clip):
    return lambda i: (jnp.clip(i - n_w, 0, n_clip - 1), 0)


def _col_chunk_map(n_w, offset):
    return lambda i: (0, offset + jnp.minimum(i, n_w - 1))


def _row_chunk_map(n_w):
    return lambda i: (jnp.minimum(i, n_w - 1), 0)


def _ffn_kernel(*refs, n_w, n_prompt_tiles, first, final):
    refs = list(refs)
    x_refs = [refs.pop(0) for _ in range(2 if first else 1)]
    g_ref, wa_ref, wb_ref, wo_ref = (refs.pop(0) for _ in range(4))
    gf_ref = refs.pop(0) if final else None
    o_refs = [refs.pop(0) for _ in range(2 if final else 1)]
    win_s, wout_s, h_ref, act_ref = (refs.pop(0) for _ in range(4))
    r_ref = refs.pop(0) if final else None

    i = pl.program_id(0)
    t = i - n_w

    @pl.when(i < n_w)
    def _load_weights():
        win_s[i, :, :CW] = wa_ref[...].astype(_BF16)
        win_s[i, :, CW:] = wb_ref[...].astype(_BF16)
        wout_s[pl.ds(pl.multiple_of(i * CW, CW), CW), :] = wo_ref[...].astype(_BF16)

    @pl.when(i >= n_w)
    def _tile():
        if first:
            @pl.when(t < n_prompt_tiles)
            def _():
                _rms_to_bf16(x_refs[0], g_ref, h_ref, o_refs[0])

            @pl.when(t >= n_prompt_tiles)
            def _():
                _rms_to_bf16(x_refs[1], g_ref, h_ref, o_refs[0])
            res_ref = o_refs[0]
        else:
            _rms_to_bf16(x_refs[0], g_ref, h_ref)
            res_ref = x_refs[0]

        for c in range(D_FF // CW):
            ab = jnp.dot(h_ref[...], win_s[c], preferred_element_type=_F32)
            a = ab[:, :CW]
            b = ab[:, CW:]
            act_ref[:, c * CW:(c + 1) * CW] = (a * jax.nn.sigmoid(a) * b).astype(_BF16)
        y = jnp.dot(act_ref[...], wout_s[...], preferred_element_type=_F32)

        if not final:
            o_refs[0][...] = res_ref[...] + 0.5 * y
        else:
            r_ref[...] = res_ref[...] + 0.5 * y

            def norm_to(o_ref):
                def body(rows):
                    o_ref[rows, :] = _rms_rows(r_ref[rows, :], gf_ref)
                _row_loop(TM, RB, 4, body)

            @pl.when(t < n_prompt_tiles)
            def _():
                norm_to(o_refs[0])

            @pl.when(t >= n_prompt_tiles)
            def _():
                norm_to(o_refs[1])


def _ffn_call(xs, g, w_in, w_out, n_prompt_tiles, gf=None):
    first = len(xs) == 2
    final = gf is not None
    n_w = D_FF // CW
    n_tiles = n_prompt_tiles + 1
    n_tok = n_tiles * TM
    tile_all = pl.BlockSpec((TM, D_MODEL), _tile_map(n_w, n_tiles))
    tile_prompt = pl.BlockSpec((TM, D_MODEL), _tile_map(n_w, n_prompt_tiles))
    tile_sample = pl.BlockSpec((TM, D_MODEL), lambda i: (0, 0))

    in_specs = [tile_prompt, tile_sample] if first else [tile_all]
    in_specs += [
        _const_spec((1, D_MODEL)),
        pl.BlockSpec((D_MODEL, CW), _col_chunk_map(n_w, 0)),
        pl.BlockSpec((D_MODEL, CW), _col_chunk_map(n_w, n_w)),
        pl.BlockSpec((CW, D_MODEL), _row_chunk_map(n_w)),
    ]
    args = list(xs) + [g, w_in, w_in, w_out]
    if final:
        in_specs.append(_const_spec((1, D_MODEL)))
        args.append(gf)
        out_specs = [tile_prompt, tile_sample]
        out_shape = [jax.ShapeDtypeStruct((n_prompt_tiles * TM, D_MODEL), _F32),
                     jax.ShapeDtypeStruct((TM, D_MODEL), _F32)]
    else:
        out_specs = tile_all
        out_shape = jax.ShapeDtypeStruct((n_tok, D_MODEL), _F32)
    scratch = [
        pltpu.VMEM((n_w, D_MODEL, 2 * CW), _BF16),
        pltpu.VMEM((D_FF, D_MODEL), _BF16),
        pltpu.VMEM((TM, D_MODEL), _BF16),
        pltpu.VMEM((TM, D_FF), _BF16),
    ]
    if final:
        scratch.append(pltpu.VMEM((TM, D_MODEL), _F32))
    return pl.pallas_call(
        functools.partial(_ffn_kernel, n_w=n_w, n_prompt_tiles=n_prompt_tiles,
                          first=first, final=final),
        grid=(n_w + n_tiles,),
        in_specs=in_specs,
        out_specs=out_specs,
        out_shape=out_shape,
        scratch_shapes=scratch,
        compiler_params=pltpu.CompilerParams(
            dimension_semantics=("arbitrary",), vmem_limit_bytes=VMEM_LIMIT),
        name="ffn_first" if first else ("ffn_final" if final else "ffn"),
    )(*args)


def _gmlp_kernel(x_ref, g_ref, wi_ref, wo_ref, vg_ref, vb_ref, ws_ref, bs_ref,
                 o_ref, vst_ref, win_s, wout_s, h_ref, uv_ref, vnb_ref, wsb_ref, y_ref,
                 *, n_w, n_prompt_tiles):
    i = pl.program_id(0)
    t = i - n_w

    @pl.when(i < n_w)
    def _load_weights():
        win_s[i] = wi_ref[...].astype(_BF16)
        wout_s[pl.ds(pl.multiple_of(i * (D_A // n_w), D_A // n_w), D_A // n_w), :] = (
            wo_ref[...].astype(_BF16))

    @pl.when(i >= n_w)
    def _tile():
        _rms_to_bf16(x_ref, g_ref, h_ref)

        ncol = 2 * D_A // n_w
        for c in range(n_w):
            tt = jnp.dot(h_ref[...], win_s[c], preferred_element_type=_F32)
            uv_ref[:, c * ncol:(c + 1) * ncol] = 0.5 * tt * (1.0 + lax.erf(tt * (2.0 ** -0.5)))

        def ln_body(rows):
            v = uv_ref[rows, D_A:]
            mu = jnp.mean(v, axis=-1, keepdims=True)
            xc = v - mu
            var = jnp.mean(xc * xc, axis=-1, keepdims=True)
            vn = xc * lax.rsqrt(var + EPS) * vg_ref[...] + vb_ref[...]
            vst_ref[0, rows, :] = vn
            vnb_ref[rows, :] = vn.astype(_BF16)
        _row_loop(TM, 16, 4, ln_body)

        row = lax.broadcasted_iota(jnp.int32, (CHUNK_A, CHUNK_A), 0)
        col = lax.broadcasted_iota(jnp.int32, (CHUNK_A, CHUNK_A), 1)
        same_stream = jnp.logical_or(t < n_prompt_tiles, (row // RB) == (col // RB))
        keep = jnp.logical_and(col <= row, same_stream)
        for gi in range(N_GROUPS_A):
            wsb_ref[gi] = jnp.where(keep, ws_ref[0, gi], 0.0).astype(_BF16)

        for n in range(TM // CHUNK_A):
            rows = slice(n * CHUNK_A, (n + 1) * CHUNK_A)
            for gi in range(N_GROUPS_A):
                cols = slice(gi * D_GROUP_A, (gi + 1) * D_GROUP_A)
                z = jnp.dot(wsb_ref[gi], vnb_ref[rows, cols], preferred_element_type=_F32)
                z = z + bs_ref[0, gi]
                y_ref[rows, cols] = (uv_ref[rows, cols] * z).astype(_BF16)

        out = jnp.dot(y_ref[...], wout_s[...], preferred_element_type=_F32)
        o_ref[...] = x_ref[...] + out


def _gmlp_call(x, g, w_in, vg, vb, ws2, bs2, w_out, n_prompt_tiles, tiles_per_stream):
    n_w = 16
    n_tiles = n_prompt_tiles + 1
    n_states = (n_tiles + tiles_per_stream - 1) // tiles_per_stream
    tile = pl.BlockSpec((TM, D_MODEL), _tile_map(n_w, n_tiles))
    group_map = lambda i: (jnp.maximum(i - n_w, 0) // n_prompt_tiles, 0, 0, 0)
    state_map = lambda i: (jnp.maximum(i - n_w, 0) // tiles_per_stream, 0, 0)
    return pl.pallas_call(
        functools.partial(_gmlp_kernel, n_w=n_w, n_prompt_tiles=n_prompt_tiles),
        grid=(n_w + n_tiles,),
        in_specs=[
            tile, _const_spec((1, D_MODEL)),
            pl.BlockSpec((D_MODEL, 2 * D_A // n_w), _col_chunk_map(n_w, 0)),
            pl.BlockSpec((D_A // n_w, D_MODEL), _row_chunk_map(n_w)),
            _const_spec((1, D_A)), _const_spec((1, D_A)),
            pl.BlockSpec((1, N_GROUPS_A, CHUNK_A, CHUNK_A), group_map),
            pl.BlockSpec((1, N_GROUPS_A, CHUNK_A, D_GROUP_A), group_map),
        ],
        out_specs=[tile, pl.BlockSpec((1, TM, D_A), state_map)],
        out_shape=[jax.ShapeDtypeStruct((n_tiles * TM, D_MODEL), _F32),
                   jax.ShapeDtypeStruct((n_states, TM, D_A), _F32)],
        scratch_shapes=[
            pltpu.VMEM((n_w, D_MODEL, 2 * D_A // n_w), _BF16),
            pltpu.VMEM((D_A, D_MODEL), _BF16),
            pltpu.VMEM((TM, D_MODEL), _BF16),
            pltpu.VMEM((TM, 2 * D_A), _F32),
            pltpu.VMEM((TM, D_A), _BF16),
            pltpu.VMEM((N_GROUPS_A, CHUNK_A, CHUNK_A), _BF16),
            pltpu.VMEM((TM, D_A), _BF16),
        ],
        compiler_params=pltpu.CompilerParams(
            dimension_semantics=("arbitrary",), vmem_limit_bytes=VMEM_LIMIT),
        name="gmlp",
    )(x, g, w_in, w_out, vg, vb, ws2, bs2)


def _conv_kernel(x_ref, g_ref, wa_ref, wb_ref, wo_ref, cache_ref, dww_ref, dwb_ref, lng_ref, lnb_ref,
                 o_ref, glu_ref, win_s, wout_s, h_ref, xp_ref, c_ref, cn_ref,
                 *, n_w, n_prompt_tiles, tiles_per_stream, n_dec, dec_seq):
    i = pl.program_id(0)
    t = i - n_w

    @pl.when(i < n_w)
    def _load_weights():
        win_s[i, :, :CW] = wa_ref[...].astype(_BF16)
        win_s[i, :, CW:] = wb_ref[...].astype(_BF16)
        wout_s[pl.ds(pl.multiple_of(i * CW, CW), CW), :] = wo_ref[...].astype(_BF16)

    def conv_block(src, dst, rb):
        n_win = rb + HIST
        for cb in range(D_CONV // 128):
            cols = slice(cb * 128, (cb + 1) * 128)
            win = xp_ref[pl.ds(src, n_win), cols]
            acc = jnp.broadcast_to(dwb_ref[:, cols], (rb, 128))
            for r in range(SUBLANES):
                offs = [o for o in range(r, HIST + 1, SUBLANES) if 0 <= o - CONV_OFF < CONV_W]
                shifted = pltpu.roll(win, n_win - r, 0) if r else win
                for o in offs:
                    k = o - CONV_OFF
                    acc = acc + shifted[o - r:o - r + rb, :] * dww_ref[k:k + 1, cols]
            c_ref[pl.ds(dst, rb), cols] = acc

    @pl.when(i >= n_w)
    def _tile():
        _rms_to_bf16(x_ref, g_ref, h_ref)

        for c in range(D_CONV // CW):
            ag = jnp.dot(h_ref[...], win_s[c], preferred_element_type=_F32)
            glu_ref[0, :, c * CW:(c + 1) * CW] = ag[:, :CW] * jax.nn.sigmoid(ag[:, CW:])

        @pl.when(t < n_prompt_tiles)
        def _prompt():
            @pl.when(t % tiles_per_stream == 0)
            def _new_stream():
                xp_ref[0:HIST, :] = jnp.zeros((HIST, D_CONV), _F32)
            xp_ref[HIST:HIST + TM, :] = glu_ref[0]

            def body(r, carry):
                base = pl.multiple_of(r * CONV_RB, CONV_RB)
                conv_block(base, base, CONV_RB)
                return carry
            lax.fori_loop(0, TM // CONV_RB, body, 0)
            xp_ref[0:HIST, :] = xp_ref[TM:TM + HIST, :]

        @pl.when(t >= n_prompt_tiles)
        def _sample():
            seg = HIST + dec_seq
            for s in range(n_dec):
                xp_ref[s * seg:s * seg + HIST, :] = cache_ref[s]
                xp_ref[s * seg + HIST:(s + 1) * seg, :] = glu_ref[0, s * dec_seq:(s + 1) * dec_seq, :]

            def body(s, carry):
                conv_block(pl.multiple_of(s * seg, seg), pl.multiple_of(s * dec_seq, dec_seq), dec_seq)
                return carry
            lax.fori_loop(0, n_dec, body, 0)

        def ln_body(rows):
            c = c_ref[rows, :]
            mu = jnp.mean(c, axis=-1, keepdims=True)
            xc = c - mu
            var = jnp.mean(xc * xc, axis=-1, keepdims=True)
            cn = xc * lax.rsqrt(var + EPS) * lng_ref[...] + lnb_ref[...]
            cn_ref[rows, :] = (cn * jax.nn.sigmoid(cn)).astype(_BF16)
        _row_loop(TM, RB, 4, ln_body)

        out = jnp.dot(cn_ref[...], wout_s[...], preferred_element_type=_F32)
        o_ref[...] = x_ref[...] + out


def _conv_call(x, g, w_in, w_out, cache_pad, dww, dwb, lng, lnb, n_prompt_tiles, tiles_per_stream):
    n_w = D_CONV // CW
    n_tiles = n_prompt_tiles + 1
    n_states = (n_tiles + tiles_per_stream - 1) // tiles_per_stream
    n_dec, hist, _ = cache_pad.shape
    dec_seq = TM // n_dec
    assert hist == HIST and dec_seq % SUBLANES == 0 and dec_seq >= CONV_W - 1
    tile = pl.BlockSpec((TM, D_MODEL), _tile_map(n_w, n_tiles))
    state_map = lambda i: (jnp.maximum(i - n_w, 0) // tiles_per_stream, 0, 0)
    xp_rows = max(TM + HIST, n_dec * (HIST + dec_seq))
    return pl.pallas_call(
        functools.partial(_conv_kernel, n_w=n_w, n_prompt_tiles=n_prompt_tiles,
                          tiles_per_stream=tiles_per_stream, n_dec=n_dec, dec_seq=dec_seq),
        grid=(n_w + n_tiles,),
        in_specs=[
            tile, _const_spec((1, D_MODEL)),
            pl.BlockSpec((D_MODEL, CW), _col_chunk_map(n_w, 0)),
            pl.BlockSpec((D_MODEL, CW), _col_chunk_map(n_w, n_w)),
            pl.BlockSpec((CW, D_MODEL), _row_chunk_map(n_w)),
            _const_spec(cache_pad.shape), _const_spec(dww.shape), _const_spec((1, D_CONV)),
            _const_spec((1, D_CONV)), _const_spec((1, D_CONV)),
        ],
        out_specs=[tile, pl.BlockSpec((1, TM, D_CONV), state_map)],
        out_shape=[jax.ShapeDtypeStruct((n_tiles * TM, D_MODEL), _F32),
                   jax.ShapeDtypeStruct((n_states, TM, D_CONV), _F32)],
        scratch_shapes=[
            pltpu.VMEM((n_w, D_MODEL, 2 * CW), _BF16),
            pltpu.VMEM((D_CONV, D_MODEL), _BF16),
            pltpu.VMEM((TM, D_MODEL), _BF16),
            pltpu.VMEM((xp_rows, D_CONV), _F32),
            pltpu.VMEM((TM, D_CONV), _F32),
            pltpu.VMEM((TM, D_CONV), _BF16),
        ],
        compiler_params=pltpu.CompilerParams(
            dimension_semantics=("arbitrary",), vmem_limit_bytes=VMEM_LIMIT),
        name="convmod",
    )(x, g, w_in, w_in, w_out, cache_pad, dww, dwb, lng, lnb)


def kernel(x_prompt, x_sample, cache_conv, norm_g, ffn_w_in, ffn_w_out, a_w_in, a_v_ln_g, a_v_ln_b,
           a_w_s, a_b_s, a_w_out, b_w_in, b_dw_w, b_dw_b, b_ln_g, b_ln_b, b_w_out, final_norm_g):
    batch, seq, _ = x_prompt.shape
    n_dec, dec_seq, _ = x_sample.shape
    depth = norm_g.shape[0]
    assert seq % TM == 0 and n_dec * dec_seq == TM and dec_seq == RB and CHUNK_A % dec_seq == 0
    tiles_per_stream = seq // TM
    n_prompt_tiles = batch * tiles_per_stream

    xs = [x_prompt.reshape(batch * seq, D_MODEL), x_sample.reshape(n_dec * dec_seq, D_MODEL)]
    a_states, conv_states = [], []
    for i in range(depth):
        j = i // 2
        xs = [_ffn_call(xs, norm_g[i, 0][None], ffn_w_in[i, 0], ffn_w_out[i, 0], n_prompt_tiles)]
        if i % 2 == 0:
            rep = CHUNK_A // dec_seq
            ws2 = jnp.stack([a_w_s[j], jnp.tile(a_w_s[j][:, :dec_seq, :dec_seq], (1, rep, rep))])
            bs_p = jnp.broadcast_to(a_b_s[j][:, :, None], (N_GROUPS_A, CHUNK_A, D_GROUP_A))
            bs_s = jnp.broadcast_to(jnp.tile(a_b_s[j][:, :dec_seq], (1, rep))[:, :, None],
                                    (N_GROUPS_A, CHUNK_A, D_GROUP_A))
            x, vst = _gmlp_call(xs[0], norm_g[i, 1][None], a_w_in[j], a_v_ln_g[j][None],
                                a_v_ln_b[j][None], ws2, jnp.stack([bs_p, bs_s]), a_w_out[j],
                                n_prompt_tiles, tiles_per_stream)
            a_states.append(vst)
        else:
            cache_pad = jnp.pad(cache_conv[j], ((0, 0), (CONV_OFF, 0), (0, 0)))
            dww = jnp.pad(b_dw_w[j], ((0, HIST - CONV_W), (0, 0)))
            x, gst = _conv_call(xs[0], norm_g[i, 1][None], b_w_in[j], b_w_out[j], cache_pad, dww,
                                b_dw_b[j][None], b_ln_g[j][None], b_ln_b[j][None],
                                n_prompt_tiles, tiles_per_stream)
            conv_states.append(gst)
        last = i == depth - 1
        res = _ffn_call([x], norm_g[i, 2][None], ffn_w_in[i, 1], ffn_w_out[i, 1], n_prompt_tiles,
                        final_norm_g[None] if last else None)
        xs = res if last else [res]

    y_prompt = xs[0].reshape(batch, seq, D_MODEL)
    y_sample = xs[1].reshape(n_dec, dec_seq, D_MODEL)
    vst = jnp.stack(a_states)
    gst = jnp.stack(conv_states)
    v_prompt = vst[:, :batch, TM - CHUNK_A:, :]
    v_sample = vst[:, batch].reshape(-1, n_dec, dec_seq, D_A)
    g_prompt = gst[:, :batch, TM - (CONV_W - 1):, :]
    g_sample = gst[:, batch].reshape(-1, n_dec, dec_seq, D_CONV)[:, :, dec_seq - (CONV_W - 1):, :]
    return (y_prompt, y_sample, v_prompt, v_sample, g_prompt, g_sample)
```

```python
import functools

import jax
import jax.numpy as jnp
from jax import lax
from jax.experimental import pallas as pl
from jax.experimental.pallas import tpu as pltpu

D_MODEL = 1024
D_FF = 2816
D_A = 2 * D_MODEL
CHUNK_A = 128
N_GROUPS_A = 8
D_GROUP_A = D_A // N_GROUPS_A
D_CONV = D_MODEL
CONV_W = 31
EPS = 1e-6

TM = 512
HM = TM // 2
RB = 32
LN_RB = 16
CONV_RB = 64
CW = 256
LANES = 128
SUBLANES = 8
HIST = 32
CONV_OFF = HIST - (CONV_W - 1)
VMEM_LIMIT = 56 * 1024 * 1024

_F32 = jnp.float32
_BF16 = jnp.bfloat16


def _const_spec(shape):
    nd = len(shape)
    return pl.BlockSpec(shape, lambda i: (0,) * nd, pipeline_mode=pl.Buffered(1))


def _blocks(start, n_rows, rb):
    return [slice(start + r * rb, start + (r + 1) * rb) for r in range(n_rows // rb)]


def _rms_rows(x, g_ref):
    ms = jnp.mean(x * x, axis=-1, keepdims=True)
    return x * lax.rsqrt(ms + EPS) * g_ref[...]


def _layer_norm_rows(x, g_ref, b_ref):
    mu = jnp.mean(x, axis=-1, keepdims=True)
    xc = x - mu
    var = jnp.mean(xc * xc, axis=-1, keepdims=True)
    return xc * lax.rsqrt(var + EPS) * g_ref[...] + b_ref[...]


def _tile_map(n_w, n_clip):
    return lambda i: (jnp.clip(i - n_w, 0, n_clip - 1), 0)


def _w_col_spec(lead, rows, cols, n_w, offset):
    none = (None,) * len(lead)
    return pl.BlockSpec(none + (rows, cols), lambda i: lead + (0, offset + jnp.minimum(i, n_w - 1)))


def _w_row_spec(lead, rows, cols, n_w):
    none = (None,) * len(lead)
    return pl.BlockSpec(none + (rows, cols), lambda i: lead + (jnp.minimum(i, n_w - 1), 0))


def _ffn_kernel(*refs, n_w, n_prompt_tiles, first, final):
    refs = list(refs)
    x_refs = [refs.pop(0) for _ in range(2 if first else 1)]
    g_ref, wa_ref, wb_ref, wo_ref = (refs.pop(0) for _ in range(4))
    gf_ref = refs.pop(0) if final else None
    o_refs = [refs.pop(0) for _ in range(2 if final else 1)]
    win_s, wout_s, h_ref, act_ref = (refs.pop(0) for _ in range(4))
    xin_ref = refs.pop(0) if first else x_refs[0]

    i = pl.program_id(0)
    is_prompt = i - n_w < n_prompt_tiles

    @pl.when(i < n_w)
    def _load_weights():
        win_s[i, :, :CW] = wa_ref[...].astype(_BF16)
        win_s[i, :, CW:] = wb_ref[...].astype(_BF16)
        wout_s[pl.ds(pl.multiple_of(i * CW, CW), CW), :] = wo_ref[...].astype(_BF16)

    @pl.when(i >= n_w)
    def _tile():
        for hh in range(TM // HM):
            for rows in _blocks(hh * HM, HM, RB):
                if first:
                    x = jnp.where(is_prompt, x_refs[0][rows, :], x_refs[1][rows, :])
                    xin_ref[rows, :] = x
                else:
                    x = xin_ref[rows, :]
                h_ref[rows, :] = _rms_rows(x, g_ref).astype(_BF16)
            rs = slice(hh * HM, (hh + 1) * HM)
            for c in range(D_FF // CW):
                ab = jnp.dot(h_ref[rs, :], win_s[c], preferred_element_type=_F32)
                a = ab[:, :CW]
                b = ab[:, CW:]
                act_ref[rs, c * CW:(c + 1) * CW] = (a * jax.nn.sigmoid(a) * b).astype(_BF16)
            y = jnp.dot(act_ref[rs, :], wout_s[...], preferred_element_type=_F32)
            if not final:
                o_refs[0][rs, :] = xin_ref[rs, :] + 0.5 * y
            else:
                o_refs[1][rs, :] = xin_ref[rs, :] + 0.5 * y
                for rows in _blocks(hh * HM, HM, RB):
                    o_refs[1][rows, :] = _rms_rows(o_refs[1][rows, :], gf_ref)
        if final:
            @pl.when(is_prompt)
            def _copy_out():
                o_refs[0][...] = o_refs[1][...]


def _ffn_call(xs, g, w_in, w_out, lead, n_prompt_tiles, gf=None):
    first = len(xs) == 2
    final = gf is not None
    n_w = D_FF // CW
    n_tiles = n_prompt_tiles + 1
    n_tok = n_tiles * TM
    tile_all = pl.BlockSpec((TM, D_MODEL), _tile_map(n_w, n_tiles))
    tile_prompt = pl.BlockSpec((TM, D_MODEL), _tile_map(n_w, n_prompt_tiles))
    tile_sample = pl.BlockSpec((TM, D_MODEL), lambda i: (0, 0))

    in_specs = [tile_prompt, tile_sample] if first else [tile_all]
    in_specs += [
        _const_spec((1, D_MODEL)),
        _w_col_spec(lead, D_MODEL, CW, n_w, 0),
        _w_col_spec(lead, D_MODEL, CW, n_w, n_w),
        _w_row_spec(lead, CW, D_MODEL, n_w),
    ]
    args = list(xs) + [g, w_in, w_in, w_out]
    if final:
        in_specs.append(_const_spec((1, D_MODEL)))
        args.append(gf)
        out_specs = [tile_prompt, tile_sample]
        out_shape = [jax.ShapeDtypeStruct((n_prompt_tiles * TM, D_MODEL), _F32),
                     jax.ShapeDtypeStruct((TM, D_MODEL), _F32)]
    else:
        out_specs = tile_all
        out_shape = jax.ShapeDtypeStruct((n_tok, D_MODEL), _F32)
    scratch = [
        pltpu.VMEM((n_w, D_MODEL, 2 * CW), _BF16),
        pltpu.VMEM((D_FF, D_MODEL), _BF16),
        pltpu.VMEM((TM, D_MODEL), _BF16),
        pltpu.VMEM((TM, D_FF), _BF16),
    ]
    if first:
        scratch.append(pltpu.VMEM((TM, D_MODEL), _F32))
    return pl.pallas_call(
        functools.partial(_ffn_kernel, n_w=n_w, n_prompt_tiles=n_prompt_tiles,
                          first=first, final=final),
        grid=(n_w + n_tiles,),
        in_specs=in_specs,
        out_specs=out_specs,
        out_shape=out_shape,
        scratch_shapes=scratch,
        compiler_params=pltpu.CompilerParams(
            dimension_semantics=("arbitrary",), vmem_limit_bytes=VMEM_LIMIT),
        name="ffn_first" if first else ("ffn_final" if final else "ffn"),
    )(*args)


def _gmlp_kernel(x_ref, g_ref, wi_ref, wo_ref, vg_ref, vb_ref, ws_ref, bs_ref,
                 o_ref, vst_ref, win_s, wout_s, h_ref, uv_ref, vnb_ref, wsb_ref, y_ref,
                 *, n_w, n_prompt_tiles):
    i = pl.program_id(0)
    t = i - n_w
    wo_rows = D_A // n_w
    ncol = 2 * D_A // n_w

    @pl.when(i < n_w)
    def _load_weights():
        win_s[i] = wi_ref[...].astype(_BF16)
        wout_s[pl.ds(pl.multiple_of(i * wo_rows, wo_rows), wo_rows), :] = wo_ref[...].astype(_BF16)

    @pl.when(i >= n_w)
    def _tile():
        row = lax.broadcasted_iota(jnp.int32, (CHUNK_A, CHUNK_A), 0)
        col = lax.broadcasted_iota(jnp.int32, (CHUNK_A, CHUNK_A), 1)
        same_stream = jnp.logical_or(t < n_prompt_tiles, (row // RB) == (col // RB))
        keep = jnp.logical_and(col <= row, same_stream)
        for gi in range(N_GROUPS_A):
            wsb_ref[gi] = jnp.where(keep, ws_ref[0, gi], 0.0).astype(_BF16)

        for hh in range(TM // HM):
            rs = slice(hh * HM, (hh + 1) * HM)
            for rows in _blocks(hh * HM, HM, RB):
                h_ref[rows, :] = _rms_rows(x_ref[rows, :], g_ref).astype(_BF16)

            for c in range(n_w):
                tt = jnp.dot(h_ref[rs, :], win_s[c], preferred_element_type=_F32)
                uv_ref[rs, c * ncol:(c + 1) * ncol] = 0.5 * tt * (1.0 + lax.erf(tt * (2.0 ** -0.5)))

            for rows in _blocks(hh * HM, HM, LN_RB):
                vn = _layer_norm_rows(uv_ref[rows, D_A:], vg_ref, vb_ref)
                vst_ref[0, rows, :] = vn
                vnb_ref[rows, :] = vn.astype(_BF16)

            for rows in _blocks(hh * HM, HM, CHUNK_A):
                for gi in range(N_GROUPS_A):
                    cols = slice(gi * D_GROUP_A, (gi + 1) * D_GROUP_A)
                    z = jnp.dot(wsb_ref[gi], vnb_ref[rows, cols], preferred_element_type=_F32)
                    z = z + bs_ref[0, gi]
                    y_ref[rows, cols] = (uv_ref[rows, cols] * z).astype(_BF16)

            out = jnp.dot(y_ref[rs, :], wout_s[...], preferred_element_type=_F32)
            o_ref[rs, :] = x_ref[rs, :] + out


def _gmlp_call(x, g, w_in, w_out, lead, vg, vb, ws2, bs2, n_prompt_tiles, tiles_per_stream):
    n_w = 16
    n_tiles = n_prompt_tiles + 1
    n_states = (n_tiles + tiles_per_stream - 1) // tiles_per_stream
    tile = pl.BlockSpec((TM, D_MODEL), _tile_map(n_w, n_tiles))
    group_map = lambda i: (jnp.maximum(i - n_w, 0) // n_prompt_tiles, 0, 0, 0)
    state_map = lambda i: (jnp.maximum(i - n_w, 0) // tiles_per_stream, 0, 0)
    return pl.pallas_call(
        functools.partial(_gmlp_kernel, n_w=n_w, n_prompt_tiles=n_prompt_tiles),
        grid=(n_w + n_tiles,),
        in_specs=[
            tile, _const_spec((1, D_MODEL)),
            _w_col_spec(lead, D_MODEL, 2 * D_A // n_w, n_w, 0),
            _w_row_spec(lead, D_A // n_w, D_MODEL, n_w),
            _const_spec((1, D_A)), _const_spec((1, D_A)),
            pl.BlockSpec((1, N_GROUPS_A, CHUNK_A, CHUNK_A), group_map),
            pl.BlockSpec((1, N_GROUPS_A, CHUNK_A, D_GROUP_A), group_map),
        ],
        out_specs=[tile, pl.BlockSpec((1, TM, D_A), state_map)],
        out_shape=[jax.ShapeDtypeStruct((n_tiles * TM, D_MODEL), _F32),
                   jax.ShapeDtypeStruct((n_states, TM, D_A), _F32)],
        scratch_shapes=[
            pltpu.VMEM((n_w, D_MODEL, 2 * D_A // n_w), _BF16),
            pltpu.VMEM((D_A, D_MODEL), _BF16),
            pltpu.VMEM((TM, D_MODEL), _BF16),
            pltpu.VMEM((TM, 2 * D_A), _F32),
            pltpu.VMEM((TM, D_A), _BF16),
            pltpu.VMEM((N_GROUPS_A, CHUNK_A, CHUNK_A), _BF16),
            pltpu.VMEM((TM, D_A), _BF16),
        ],
        compiler_params=pltpu.CompilerParams(
            dimension_semantics=("arbitrary",), vmem_limit_bytes=VMEM_LIMIT),
        name="gmlp",
    )(x, g, w_in, w_out, vg, vb, ws2, bs2)


def _conv_kernel(x_ref, g_ref, wa_ref, wb_ref, wo_ref, cache_ref, dww_ref, dwb_ref, lng_ref, lnb_ref,
                 o_ref, glu_ref, win_s, wout_s, h_ref, xp_ref, c_ref, cn_ref,
                 *, n_w, n_prompt_tiles, tiles_per_stream, n_dec, dec_seq):
    i = pl.program_id(0)
    t = i - n_w

    @pl.when(i < n_w)
    def _load_weights():
        win_s[i, :, :CW] = wa_ref[...].astype(_BF16)
        win_s[i, :, CW:] = wb_ref[...].astype(_BF16)
        wout_s[pl.ds(pl.multiple_of(i * CW, CW), CW), :] = wo_ref[...].astype(_BF16)

    def conv_block(src, dst, rb):
        n_win = rb + HIST
        for cb in range(D_CONV // LANES):
            cols = slice(cb * LANES, (cb + 1) * LANES)
            win = xp_ref[pl.ds(src, n_win), cols]
            acc = jnp.broadcast_to(dwb_ref[:, cols], (rb, LANES))
            for r in range(SUBLANES):
                offs = [o for o in range(r, HIST + 1, SUBLANES) if 0 <= o - CONV_OFF < CONV_W]
                shifted = pltpu.roll(win, n_win - r, 0) if r else win
                for o in offs:
                    k = o - CONV_OFF
                    acc = acc + shifted[o - r:o - r + rb, :] * dww_ref[k:k + 1, cols]
            c_ref[pl.ds(dst, rb), cols] = acc

    def rms_rows(rows):
        h_ref[rows, :] = _rms_rows(x_ref[rows, :], g_ref).astype(_BF16)

    def in_proj(rs, xp_row0):
        for c in range(D_CONV // CW):
            ag = jnp.dot(h_ref[rs, :], win_s[c], preferred_element_type=_F32)
            glu = ag[:, :CW] * jax.nn.sigmoid(ag[:, CW:])
            glu_ref[0, rs, c * CW:(c + 1) * CW] = glu
            if xp_row0 is not None:
                xp_ref[xp_row0:xp_row0 + rs.stop - rs.start, c * CW:(c + 1) * CW] = glu

    def ln_rows(rows):
        cn = _layer_norm_rows(c_ref[rows, :], lng_ref, lnb_ref)
        cn_ref[rows, :] = (cn * jax.nn.sigmoid(cn)).astype(_BF16)

    def out_proj(rs):
        out = jnp.dot(cn_ref[rs, :], wout_s[...], preferred_element_type=_F32)
        o_ref[rs, :] = x_ref[rs, :] + out

    @pl.when(jnp.logical_and(i >= n_w, t < n_prompt_tiles))
    def _prompt_tile():
        @pl.when(t % tiles_per_stream == 0)
        def _new_stream():
            xp_ref[0:HIST, :] = jnp.zeros((HIST, D_CONV), _F32)

        for hh in range(TM // HM):
            rs = slice(hh * HM, (hh + 1) * HM)
            for rows in _blocks(hh * HM, HM, RB):
                rms_rows(rows)
            in_proj(rs, HIST + hh * HM)
            for rows in _blocks(hh * HM, HM, CONV_RB):
                conv_block(rows.start, rows.start, CONV_RB)
            for rows in _blocks(hh * HM, HM, RB):
                ln_rows(rows)
            out_proj(rs)
        xp_ref[0:HIST, :] = xp_ref[TM:TM + HIST, :]

    @pl.when(t >= n_prompt_tiles)
    def _sample_tile():
        def loop(n, rb, body):
            def step(r, carry):
                body(pl.ds(pl.multiple_of(r * rb, rb), rb))
                return carry
            lax.fori_loop(0, n, step, 0)

        loop(TM // RB, RB, rms_rows)
        in_proj(slice(0, TM), None)
        seg = HIST + dec_seq
        for s in range(n_dec):
            xp_ref[s * seg:s * seg + HIST, :] = cache_ref[s]
            xp_ref[s * seg + HIST:(s + 1) * seg, :] = glu_ref[0, s * dec_seq:(s + 1) * dec_seq, :]

        def seg_body(s, carry):
            conv_block(pl.multiple_of(s * seg, seg), pl.multiple_of(s * dec_seq, dec_seq), dec_seq)
            return carry
        lax.fori_loop(0, n_dec, seg_body, 0)
        loop(TM // RB, RB, ln_rows)
        out_proj(slice(0, TM))


def _conv_call(x, g, w_in, w_out, lead, cache_pad, dww, dwb, lng, lnb, n_prompt_tiles,
               tiles_per_stream):
    n_w = D_CONV // CW
    n_tiles = n_prompt_tiles + 1
    n_states = (n_tiles + tiles_per_stream - 1) // tiles_per_stream
    n_dec, hist, _ = cache_pad.shape
    dec_seq = TM // n_dec
    assert hist == HIST and dec_seq % SUBLANES == 0 and dec_seq >= CONV_W - 1
    tile = pl.BlockSpec((TM, D_MODEL), _tile_map(n_w, n_tiles))
    state_map = lambda i: (jnp.maximum(i - n_w, 0) // tiles_per_stream, 0, 0)
    xp_rows = max(TM + HIST, n_dec * (HIST + dec_seq))
    return pl.pallas_call(
        functools.partial(_conv_kernel, n_w=n_w, n_prompt_tiles=n_prompt_tiles,
                          tiles_per_stream=tiles_per_stream, n_dec=n_dec, dec_seq=dec_seq),
        grid=(n_w + n_tiles,),
        in_specs=[
            tile, _const_spec((1, D_MODEL)),
            _w_col_spec(lead, D_MODEL, CW, n_w, 0),
            _w_col_spec(lead, D_MODEL, CW, n_w, n_w),
            _w_row_spec(lead, CW, D_MODEL, n_w),
            _const_spec(cache_pad.shape), _const_spec(dww.shape), _const_spec((1, D_CONV)),
            _const_spec((1, D_CONV)), _const_spec((1, D_CONV)),
        ],
        out_specs=[tile, pl.BlockSpec((1, TM, D_CONV), state_map)],
        out_shape=[jax.ShapeDtypeStruct((n_tiles * TM, D_MODEL), _F32),
                   jax.ShapeDtypeStruct((n_states, TM, D_CONV), _F32)],
        scratch_shapes=[
            pltpu.VMEM((n_w, D_MODEL, 2 * CW), _BF16),
            pltpu.VMEM((D_CONV, D_MODEL), _BF16),
            pltpu.VMEM((TM, D_MODEL), _BF16),
            pltpu.VMEM((xp_rows, D_CONV), _F32),
            pltpu.VMEM((TM, D_CONV), _F32),
            pltpu.VMEM((TM, D_CONV), _BF16),
        ],
        compiler_params=pltpu.CompilerParams(
            dimension_semantics=("arbitrary",), vmem_limit_bytes=VMEM_LIMIT),
        name="convmod",
    )(x, g, w_in, w_in, w_out, cache_pad, dww, dwb, lng, lnb)


def kernel(x_prompt, x_sample, cache_conv, norm_g, ffn_w_in, ffn_w_out, a_w_in, a_v_ln_g, a_v_ln_b,
           a_w_s, a_b_s, a_w_out, b_w_in, b_dw_w, b_dw_b, b_ln_g, b_ln_b, b_w_out, final_norm_g):
    batch, seq, _ = x_prompt.shape
    n_dec, dec_seq, _ = x_sample.shape
    depth = norm_g.shape[0]
    assert seq % TM == 0 and n_dec * dec_seq == TM and dec_seq == RB and CHUNK_A % dec_seq == 0
    tiles_per_stream = seq // TM
    n_prompt_tiles = batch * tiles_per_stream

    xs = [x_prompt.reshape(batch * seq, D_MODEL), x_sample.reshape(n_dec * dec_seq, D_MODEL)]
    a_states, conv_states = [], []
    for i in range(depth):
        j = i // 2
        xs = [_ffn_call(xs, norm_g[i, 0][None], ffn_w_in, ffn_w_out, (i, 0), n_prompt_tiles)]
        if i % 2 == 0:
            rep = CHUNK_A // dec_seq
            ws2 = jnp.stack([a_w_s[j], jnp.tile(a_w_s[j][:, :dec_seq, :dec_seq], (1, rep, rep))])
            bs_p = jnp.broadcast_to(a_b_s[j][:, :, None], (N_GROUPS_A, CHUNK_A, D_GROUP_A))
            bs_s = jnp.broadcast_to(jnp.tile(a_b_s[j][:, :dec_seq], (1, rep))[:, :, None],
                                    (N_GROUPS_A, CHUNK_A, D_GROUP_A))
            x, vst = _gmlp_call(xs[0], norm_g[i, 1][None], a_w_in, a_w_out, (j,), a_v_ln_g[j][None],
                                a_v_ln_b[j][None], ws2, jnp.stack([bs_p, bs_s]),
                                n_prompt_tiles, tiles_per_stream)
            a_states.append(vst)
        else:
            cache_pad = jnp.pad(cache_conv[j], ((0, 0), (CONV_OFF, 0), (0, 0)))
            dww = jnp.pad(b_dw_w[j], ((0, HIST - CONV_W), (0, 0)))
            x, gst = _conv_call(xs[0], norm_g[i, 1][None], b_w_in, b_w_out, (j,), cache_pad, dww,
                                b_dw_b[j][None], b_ln_g[j][None], b_ln_b[j][None],
                                n_prompt_tiles, tiles_per_stream)
            conv_states.append(gst)
        last = i == depth - 1
        res = _ffn_call([x], norm_g[i, 2][None], ffn_w_in, ffn_w_out, (i, 1), n_prompt_tiles,
                        final_norm_g[None] if last else None)
        xs = res if last else [res]

    y_prompt = xs[0].reshape(batch, seq, D_MODEL)
    y_sample = xs[1].reshape(n_dec, dec_seq, D_MODEL)
    vst = jnp.stack(a_states)
    gst = jnp.stack(conv_states)
    v_prompt = vst[:, :batch, TM - CHUNK_A:, :]
    v_sample = vst[:, batch].reshape(-1, n_dec, dec_seq, D_A)
    g_prompt = gst[:, :batch, TM - (CONV_W - 1):, :]
    g_sample = gst[:, batch].reshape(-1, n_dec, dec_seq, D_CONV)[:, :, dec_seq - (CONV_W - 1):, :]
    return (y_prompt, y_sample, v_prompt, v_sample, g_prompt, g_sample)
```

```python
import functools

import jax
import jax.numpy as jnp
from jax import lax
from jax.experimental import pallas as pl
from jax.experimental.pallas import tpu as pltpu

D_MODEL = 1024
D_FF = 2816
D_A = 2 * D_MODEL
CHUNK_A = 128
N_GROUPS_A = 8
D_GROUP_A = D_A // N_GROUPS_A
D_CONV = D_MODEL
CONV_W = 31
EPS = 1e-6

TM = 512
HM = TM // 2
RB = 32
LN_RB = 16
CW = 256
LANES = 128
SUBLANES = 8
HIST = 32
CONV_OFF = HIST - (CONV_W - 1)
VMEM_LIMIT = 56 * 1024 * 1024

_F32 = jnp.float32
_BF16 = jnp.bfloat16


def _const_spec(shape):
    nd = len(shape)
    return pl.BlockSpec(shape, lambda i: (0,) * nd, pipeline_mode=pl.Buffered(1))


def _blocks(start, n_rows, rb):
    return [slice(start + r * rb, start + (r + 1) * rb) for r in range(n_rows // rb)]


def _rms_rows(x, g_ref):
    ms = jnp.mean(x * x, axis=-1, keepdims=True)
    return x * lax.rsqrt(ms + EPS) * g_ref[...]


def _layer_norm_rows(x, g_ref, b_ref):
    mu = jnp.mean(x, axis=-1, keepdims=True)
    xc = x - mu
    var = jnp.mean(xc * xc, axis=-1, keepdims=True)
    return xc * lax.rsqrt(var + EPS) * g_ref[...] + b_ref[...]


def _zero_after(x):
    bits = pltpu.bitcast(x[:SUBLANES, :LANES], jnp.uint32)
    return ((bits >> 16) >> 16).astype(_F32)


def _tile_map(n_w, n_clip):
    return lambda i: (jnp.clip(i - n_w, 0, n_clip - 1), 0)


def _w_col_spec(lead, rows, cols, n_w, offset):
    none = (None,) * len(lead)
    return pl.BlockSpec(none + (rows, cols), lambda i: lead + (0, offset + jnp.minimum(i, n_w - 1)))


def _w_row_spec(lead, rows, cols, n_w):
    none = (None,) * len(lead)
    return pl.BlockSpec(none + (rows, cols), lambda i: lead + (jnp.minimum(i, n_w - 1), 0))


def _ffn_kernel(*refs, n_w, n_prompt_tiles, first):
    refs = list(refs)
    x_refs = [refs.pop(0) for _ in range(2 if first else 1)]
    g_ref, wa_ref, wb_ref, wo_ref, o_ref, win_s, wout_s, h_ref, act_ref = (refs.pop(0) for _ in range(9))
    xin_ref = refs.pop(0) if first else x_refs[0]

    i = pl.program_id(0)
    is_prompt = i - n_w < n_prompt_tiles

    @pl.when(i < n_w)
    def _load_weights():
        win_s[i, :, :CW] = wa_ref[...].astype(_BF16)
        win_s[i, :, CW:] = wb_ref[...].astype(_BF16)
        wout_s[pl.ds(pl.multiple_of(i * CW, CW), CW), :] = wo_ref[...].astype(_BF16)

    @pl.when(i >= n_w)
    def _tile():
        for hh in range(TM // HM):
            for rows in _blocks(hh * HM, HM, RB):
                if first:
                    x = jnp.where(is_prompt, x_refs[0][rows, :], x_refs[1][rows, :])
                    xin_ref[rows, :] = x
                else:
                    x = xin_ref[rows, :]
                h_ref[rows, :] = _rms_rows(x, g_ref).astype(_BF16)
            rs = slice(hh * HM, (hh + 1) * HM)
            for c in range(D_FF // CW):
                ab = jnp.dot(h_ref[rs, :], win_s[c], preferred_element_type=_F32)
                a = ab[:, :CW]
                b = ab[:, CW:]
                act_ref[rs, c * CW:(c + 1) * CW] = (a * jax.nn.sigmoid(a) * b).astype(_BF16)
            y = jnp.dot(act_ref[rs, :], wout_s[...], preferred_element_type=_F32)
            o_ref[rs, :] = xin_ref[rs, :] + 0.5 * y


def _ffn_call(xs, g, w_in, w_out, lead, n_prompt_tiles):
    first = len(xs) == 2
    n_w = D_FF // CW
    n_tiles = n_prompt_tiles + 1
    n_tok = n_tiles * TM
    tile_all = pl.BlockSpec((TM, D_MODEL), _tile_map(n_w, n_tiles))
    tile_prompt = pl.BlockSpec((TM, D_MODEL), _tile_map(n_w, n_prompt_tiles))
    tile_sample = pl.BlockSpec((TM, D_MODEL), lambda i: (0, 0))

    in_specs = [tile_prompt, tile_sample] if first else [tile_all]
    in_specs += [
        _const_spec((1, D_MODEL)),
        _w_col_spec(lead, D_MODEL, CW, n_w, 0),
        _w_col_spec(lead, D_MODEL, CW, n_w, n_w),
        _w_row_spec(lead, CW, D_MODEL, n_w),
    ]
    args = list(xs) + [g, w_in, w_in, w_out]
    scratch = [
        pltpu.VMEM((n_w, D_MODEL, 2 * CW), _BF16),
        pltpu.VMEM((D_FF, D_MODEL), _BF16),
        pltpu.VMEM((TM, D_MODEL), _BF16),
        pltpu.VMEM((TM, D_FF), _BF16),
    ]
    if first:
        scratch.append(pltpu.VMEM((TM, D_MODEL), _F32))
    return pl.pallas_call(
        functools.partial(_ffn_kernel, n_w=n_w, n_prompt_tiles=n_prompt_tiles, first=first),
        grid=(n_w + n_tiles,),
        in_specs=in_specs,
        out_specs=tile_all,
        out_shape=jax.ShapeDtypeStruct((n_tok, D_MODEL), _F32),
        scratch_shapes=scratch,
        compiler_params=pltpu.CompilerParams(
            dimension_semantics=("arbitrary",), vmem_limit_bytes=VMEM_LIMIT),
        name="ffn_first" if first else "ffn",
    )(*args)


def _gmlp_kernel(x_ref, g_ref, wi_ref, wo_ref, vg_ref, vb_ref, ws_ref, bs_ref,
                 o_ref, vst_ref, win_s, wout_s, h_ref, uv_ref, vnb_ref, wsb_ref, y_ref,
                 *, n_w, n_prompt_tiles):
    i = pl.program_id(0)
    t = i - n_w
    wo_rows = D_A // n_w
    ncol = 2 * D_A // n_w

    @pl.when(i < n_w)
    def _load_weights():
        win_s[i] = wi_ref[...].astype(_BF16)
        wout_s[pl.ds(pl.multiple_of(i * wo_rows, wo_rows), wo_rows), :] = wo_ref[...].astype(_BF16)

    @pl.when(i >= n_w)
    def _tile():
        row = lax.broadcasted_iota(jnp.int32, (CHUNK_A, CHUNK_A), 0)
        col = lax.broadcasted_iota(jnp.int32, (CHUNK_A, CHUNK_A), 1)
        same_stream = jnp.logical_or(t < n_prompt_tiles, (row // RB) == (col // RB))
        keep = jnp.logical_and(col <= row, same_stream)
        for gi in range(N_GROUPS_A):
            wsb_ref[gi] = jnp.where(keep, ws_ref[0, gi], 0.0).astype(_BF16)

        for hh in range(TM // HM):
            rs = slice(hh * HM, (hh + 1) * HM)
            for rows in _blocks(hh * HM, HM, RB):
                h_ref[rows, :] = _rms_rows(x_ref[rows, :], g_ref).astype(_BF16)

            for c in range(n_w):
                tt = jnp.dot(h_ref[rs, :], win_s[c], preferred_element_type=_F32)
                uv_ref[rs, c * ncol:(c + 1) * ncol] = 0.5 * tt * (1.0 + lax.erf(tt * (2.0 ** -0.5)))

            for rows in _blocks(hh * HM, HM, LN_RB):
                vn = _layer_norm_rows(uv_ref[rows, D_A:], vg_ref, vb_ref)
                vst_ref[0, rows, :] = vn
                vnb_ref[rows, :] = vn.astype(_BF16)

            for rows in _blocks(hh * HM, HM, CHUNK_A):
                for gi in range(N_GROUPS_A):
                    cols = slice(gi * D_GROUP_A, (gi + 1) * D_GROUP_A)
                    z = jnp.dot(wsb_ref[gi], vnb_ref[rows, cols], preferred_element_type=_F32)
                    z = z + bs_ref[0, gi]
                    y_ref[rows, cols] = (uv_ref[rows, cols] * z).astype(_BF16)

            out = jnp.dot(y_ref[rs, :], wout_s[...], preferred_element_type=_F32)
            o_ref[rs, :] = x_ref[rs, :] + out


def _gmlp_call(x, g, w_in, w_out, lead, vg, vb, ws2, bs2, n_prompt_tiles, tiles_per_stream):
    n_w = 16
    n_tiles = n_prompt_tiles + 1
    n_states = (n_tiles + tiles_per_stream - 1) // tiles_per_stream
    tile = pl.BlockSpec((TM, D_MODEL), _tile_map(n_w, n_tiles))
    group_map = lambda i: (jnp.maximum(i - n_w, 0) // n_prompt_tiles, 0, 0, 0)
    state_map = lambda i: (jnp.maximum(i - n_w, 0) // tiles_per_stream, 0, 0)
    return pl.pallas_call(
        functools.partial(_gmlp_kernel, n_w=n_w, n_prompt_tiles=n_prompt_tiles),
        grid=(n_w + n_tiles,),
        in_specs=[
            tile, _const_spec((1, D_MODEL)),
            _w_col_spec(lead, D_MODEL, 2 * D_A // n_w, n_w, 0),
            _w_row_spec(lead, D_A // n_w, D_MODEL, n_w),
            _const_spec((1, D_A)), _const_spec((1, D_A)),
            pl.BlockSpec((1, N_GROUPS_A, CHUNK_A, CHUNK_A), group_map),
            pl.BlockSpec((1, N_GROUPS_A, CHUNK_A, D_GROUP_A), group_map),
        ],
        out_specs=[tile, pl.BlockSpec((1, TM, D_A), state_map)],
        out_shape=[jax.ShapeDtypeStruct((n_tiles * TM, D_MODEL), _F32),
                   jax.ShapeDtypeStruct((n_states, TM, D_A), _F32)],
        scratch_shapes=[
            pltpu.VMEM((n_w, D_MODEL, 2 * D_A // n_w), _BF16),
            pltpu.VMEM((D_A, D_MODEL), _BF16),
            pltpu.VMEM((TM, D_MODEL), _BF16),
            pltpu.VMEM((TM, 2 * D_A), _F32),
            pltpu.VMEM((TM, D_A), _BF16),
            pltpu.VMEM((N_GROUPS_A, CHUNK_A, CHUNK_A), _BF16),
            pltpu.VMEM((TM, D_A), _BF16),
        ],
        compiler_params=pltpu.CompilerParams(
            dimension_semantics=("arbitrary",), vmem_limit_bytes=VMEM_LIMIT),
        name="gmlp",
    )(x, g, w_in, w_out, vg, vb, ws2, bs2)


def _ffn_glu_kernel(x_ref, g_ref, wa_ref, wb_ref, wo_ref, gm_ref, ca_ref, cb_ref,
                    o_ref, glu_ref, win_s, wout_s, cwin_s, h_ref, act_ref, *, n_w, n_wc):
    i = pl.program_id(0)

    @pl.when(i < n_w)
    def _load_weights():
        win_s[i, :, :CW] = wa_ref[...].astype(_BF16)
        win_s[i, :, CW:] = wb_ref[...].astype(_BF16)
        wout_s[pl.ds(pl.multiple_of(i * CW, CW), CW), :] = wo_ref[...].astype(_BF16)

    @pl.when(i < n_wc)
    def _load_conv_weights():
        cwin_s[i, :, :CW] = ca_ref[...].astype(_BF16)
        cwin_s[i, :, CW:] = cb_ref[...].astype(_BF16)

    @pl.when(i >= n_w)
    def _tile():
        for hh in range(TM // HM):
            rs = slice(hh * HM, (hh + 1) * HM)
            for rows in _blocks(hh * HM, HM, RB):
                h_ref[rows, :] = _rms_rows(x_ref[rows, :], g_ref).astype(_BF16)
            for c in range(D_FF // CW):
                ab = jnp.dot(h_ref[rs, :], win_s[c], preferred_element_type=_F32)
                a = ab[:, :CW]
                b = ab[:, CW:]
                act_ref[rs, c * CW:(c + 1) * CW] = (a * jax.nn.sigmoid(a) * b).astype(_BF16)
            y = jnp.dot(act_ref[rs, :], wout_s[...], preferred_element_type=_F32)
            o_ref[rs, :] = x_ref[rs, :] + 0.5 * y
            for rows in _blocks(hh * HM, HM, RB):
                h_ref[rows, :] = _rms_rows(o_ref[rows, :], gm_ref).astype(_BF16)
            for c in range(D_CONV // CW):
                ag = jnp.dot(h_ref[rs, :], cwin_s[c], preferred_element_type=_F32)
                glu_ref[rs, c * CW:(c + 1) * CW] = ag[:, :CW] * jax.nn.sigmoid(ag[:, CW:])


def _ffn_glu_call(x, g, w_in, w_out, lead, gm, cw_in, clead, n_prompt_tiles):
    n_w = D_FF // CW
    n_wc = D_CONV // CW
    n_tiles = n_prompt_tiles + 1
    tile = pl.BlockSpec((TM, D_MODEL), _tile_map(n_w, n_tiles))
    return pl.pallas_call(
        functools.partial(_ffn_glu_kernel, n_w=n_w, n_wc=n_wc),
        grid=(n_w + n_tiles,),
        in_specs=[
            tile, _const_spec((1, D_MODEL)),
            _w_col_spec(lead, D_MODEL, CW, n_w, 0),
            _w_col_spec(lead, D_MODEL, CW, n_w, n_w),
            _w_row_spec(lead, CW, D_MODEL, n_w),
            _const_spec((1, D_MODEL)),
            _w_col_spec(clead, D_MODEL, CW, n_wc, 0),
            _w_col_spec(clead, D_MODEL, CW, n_wc, n_wc),
        ],
        out_specs=[tile, tile],
        out_shape=[jax.ShapeDtypeStruct((n_tiles * TM, D_MODEL), _F32),
                   jax.ShapeDtypeStruct((n_tiles * TM, D_CONV), _F32)],
        scratch_shapes=[
            pltpu.VMEM((n_w, D_MODEL, 2 * CW), _BF16),
            pltpu.VMEM((D_FF, D_MODEL), _BF16),
            pltpu.VMEM((n_wc, D_MODEL, 2 * CW), _BF16),
            pltpu.VMEM((TM, D_MODEL), _BF16),
            pltpu.VMEM((TM, D_FF), _BF16),
        ],
        compiler_params=pltpu.CompilerParams(
            dimension_semantics=("arbitrary",), vmem_limit_bytes=VMEM_LIMIT),
        name="ffn_glu",
    )(x, g, w_in, w_in, w_out, gm, cw_in, cw_in)


def _conv_ffn_kernel(x_ref, glu_ref, g_ref, wa_ref, wb_ref, wo_ref, cwo_ref, cache_ref, dww_ref, dwb_ref,
                     lng_ref, lnb_ref, gf_ref, op_ref, os_ref,
                     win_s, wout_s, cwout_s, hist_ref, c_ref, cn_ref, x2_ref, h_ref, act_ref,
                     *, n_w, n_wc, n_prompt_tiles, tiles_per_stream, n_dec, dec_seq):
    i = pl.program_id(0)
    t = i - n_w
    n_win = dec_seq + HIST

    @pl.when(i < n_w)
    def _load_weights():
        win_s[i, :, :CW] = wa_ref[...].astype(_BF16)
        win_s[i, :, CW:] = wb_ref[...].astype(_BF16)
        wout_s[pl.ds(pl.multiple_of(i * CW, CW), CW), :] = wo_ref[...].astype(_BF16)

    @pl.when(i < n_wc)
    def _load_conv_weights():
        cwout_s[pl.ds(pl.multiple_of(i * CW, CW), CW), :] = cwo_ref[...].astype(_BF16)

    @pl.when(i == 0)
    def _init():
        hist_ref[...] = jnp.zeros((HIST, D_CONV), _F32)

    def conv_units(tile):
        is_prompt = tile < n_prompt_tiles
        new_stream = tile % tiles_per_stream == 0

        def unit(s, cb, zero=None):
            cols = slice(cb * LANES, (cb + 1) * LANES)
            cur = glu_ref[s * dec_seq:(s + 1) * dec_seq, cols]
            if s == 0:
                prev = jnp.where(new_stream, 0.0, hist_ref[:, cols])
            else:
                prev = glu_ref[s * dec_seq - HIST:s * dec_seq, cols]
            win = jnp.concatenate([jnp.where(is_prompt, prev, cache_ref[s, :, cols]), cur], axis=0)
            if zero is not None:
                win = win + jnp.concatenate([zero] * (n_win // SUBLANES), axis=0)
            acc = jnp.broadcast_to(dwb_ref[:, cols], (dec_seq, LANES))
            for r in range(SUBLANES):
                offs = [o for o in range(r, HIST + 1, SUBLANES) if 0 <= o - CONV_OFF < CONV_W]
                shifted = pltpu.roll(win, n_win - r, 0) if r else win
                for o in offs:
                    k = o - CONV_OFF
                    acc = acc + shifted[o - r:o - r + dec_seq, :] * dww_ref[k:k + 1, cols]
            c_ref[s * dec_seq:(s + 1) * dec_seq, cols] = acc
            return acc

        def save_history(zero=None):
            hist_ref[...] = glu_ref[TM - HIST:TM, :]
            return None

        units = [functools.partial(unit, s, cb) for s in range(n_dec) for cb in range(D_CONV // LANES)]
        return units + [save_history]

    @pl.when(i == n_w - 1)
    def _first_conv():
        for u in conv_units(0):
            u()

    @pl.when(i >= n_w)
    def _tile():
        pending = conv_units(t + 1)
        done = []

        def filler(n, after):
            zero = _zero_after(after)
            for _ in range(min(n, len(pending))):
                z = zero
                if done and done[-1] is not None:
                    z = z + _zero_after(done[-1])
                done.append(pending.pop(0)(z))

        for hh in range(TM // HM):
            rs = slice(hh * HM, (hh + 1) * HM)
            for rows in _blocks(hh * HM, HM, RB):
                cn = _layer_norm_rows(c_ref[rows, :], lng_ref, lnb_ref)
                cn_ref[rows, :] = (cn * jax.nn.sigmoid(cn)).astype(_BF16)
            out = jnp.dot(cn_ref[rs, :], cwout_s[...], preferred_element_type=_F32)
            x2_ref[rs, :] = x_ref[rs, :] + out
            filler(4, out)
            for rows in _blocks(hh * HM, HM, RB):
                h_ref[rows, :] = _rms_rows(x2_ref[rows, :], g_ref).astype(_BF16)
            for c in range(D_FF // CW):
                ab = jnp.dot(h_ref[rs, :], win_s[c], preferred_element_type=_F32)
                a = ab[:, :CW]
                b = ab[:, CW:]
                act_ref[rs, c * CW:(c + 1) * CW] = (a * jax.nn.sigmoid(a) * b).astype(_BF16)
                filler(4, ab)
            y = jnp.dot(act_ref[rs, :], wout_s[...], preferred_element_type=_F32)
            os_ref[rs, :] = x2_ref[rs, :] + 0.5 * y
            filler(16, y)
            for rows in _blocks(hh * HM, HM, RB):
                os_ref[rows, :] = _rms_rows(os_ref[rows, :], gf_ref)
        filler(len(pending), y)

        @pl.when(t < n_prompt_tiles)
        def _copy_out():
            op_ref[...] = os_ref[...]


def _conv_ffn_call(x, glu, g, w_in, w_out, lead, cw_out, clead, cache_pad, dww, dwb, lng, lnb, gf,
                   n_prompt_tiles, tiles_per_stream):
    n_w = D_FF // CW
    n_wc = D_CONV // CW
    n_tiles = n_prompt_tiles + 1
    n_dec, hist, _ = cache_pad.shape
    dec_seq = TM // n_dec
    assert hist == HIST and dec_seq == HIST and dec_seq >= CONV_W - 1
    tile = pl.BlockSpec((TM, D_MODEL), _tile_map(n_w, n_tiles))
    next_tile = pl.BlockSpec((TM, D_CONV), _tile_map(n_w - 1, n_tiles))
    tile_prompt = pl.BlockSpec((TM, D_MODEL), _tile_map(n_w, n_prompt_tiles))
    tile_sample = pl.BlockSpec((TM, D_MODEL), lambda i: (0, 0))
    return pl.pallas_call(
        functools.partial(_conv_ffn_kernel, n_w=n_w, n_wc=n_wc, n_prompt_tiles=n_prompt_tiles,
                          tiles_per_stream=tiles_per_stream, n_dec=n_dec, dec_seq=dec_seq),
        grid=(n_w + n_tiles,),
        in_specs=[
            tile, next_tile, _const_spec((1, D_MODEL)),
            _w_col_spec(lead, D_MODEL, CW, n_w, 0),
            _w_col_spec(lead, D_MODEL, CW, n_w, n_w),
            _w_row_spec(lead, CW, D_MODEL, n_w),
            _w_row_spec(clead, CW, D_MODEL, n_wc),
            _const_spec(cache_pad.shape), _const_spec(dww.shape), _const_spec((1, D_CONV)),
            _const_spec((1, D_CONV)), _const_spec((1, D_CONV)), _const_spec((1, D_MODEL)),
        ],
        out_specs=[tile_prompt, tile_sample],
        out_shape=[jax.ShapeDtypeStruct((n_prompt_tiles * TM, D_MODEL), _F32),
                   jax.ShapeDtypeStruct((TM, D_MODEL), _F32)],
        scratch_shapes=[
            pltpu.VMEM((n_w, D_MODEL, 2 * CW), _BF16),
            pltpu.VMEM((D_FF, D_MODEL), _BF16),
            pltpu.VMEM((D_CONV, D_MODEL), _BF16),
            pltpu.VMEM((HIST, D_CONV), _F32),
            pltpu.VMEM((TM, D_CONV), _F32),
            pltpu.VMEM((TM, D_CONV), _BF16),
            pltpu.VMEM((TM, D_MODEL), _F32),
            pltpu.VMEM((TM, D_MODEL), _BF16),
            pltpu.VMEM((TM, D_FF), _BF16),
        ],
        compiler_params=pltpu.CompilerParams(
            dimension_semantics=("arbitrary",), vmem_limit_bytes=VMEM_LIMIT),
        name="conv_ffn",
    )(x, glu, g, w_in, w_in, w_out, cw_out, cache_pad, dww, dwb, lng, lnb, gf)


def kernel(x_prompt, x_sample, cache_conv, norm_g, ffn_w_in, ffn_w_out, a_w_in, a_v_ln_g, a_v_ln_b,
           a_w_s, a_b_s, a_w_out, b_w_in, b_dw_w, b_dw_b, b_ln_g, b_ln_b, b_w_out, final_norm_g):
    batch, seq, _ = x_prompt.shape
    n_dec, dec_seq, _ = x_sample.shape
    depth = norm_g.shape[0]
    assert depth == 2 and seq % TM == 0 and n_dec * dec_seq == TM and dec_seq == RB
    assert CHUNK_A % dec_seq == 0
    tiles_per_stream = seq // TM
    n_prompt_tiles = batch * tiles_per_stream

    xs = [x_prompt.reshape(batch * seq, D_MODEL), x_sample.reshape(n_dec * dec_seq, D_MODEL)]
    x = _ffn_call(xs, norm_g[0, 0][None], ffn_w_in, ffn_w_out, (0, 0), n_prompt_tiles)
    rep = CHUNK_A // dec_seq
    ws2 = jnp.stack([a_w_s[0], jnp.tile(a_w_s[0][:, :dec_seq, :dec_seq], (1, rep, rep))])
    bs_p = jnp.broadcast_to(a_b_s[0][:, :, None], (N_GROUPS_A, CHUNK_A, D_GROUP_A))
    bs_s = jnp.broadcast_to(jnp.tile(a_b_s[0][:, :dec_seq], (1, rep))[:, :, None],
                            (N_GROUPS_A, CHUNK_A, D_GROUP_A))
    x, vst = _gmlp_call(x, norm_g[0, 1][None], a_w_in, a_w_out, (0,), a_v_ln_g[0][None],
                        a_v_ln_b[0][None], ws2, jnp.stack([bs_p, bs_s]),
                        n_prompt_tiles, tiles_per_stream)
    x = _ffn_call([x], norm_g[0, 2][None], ffn_w_in, ffn_w_out, (0, 1), n_prompt_tiles)
    x, glu = _ffn_glu_call(x, norm_g[1, 0][None], ffn_w_in, ffn_w_out, (1, 0), norm_g[1, 1][None],
                           b_w_in, (0,), n_prompt_tiles)
    cache_pad = jnp.pad(cache_conv[0], ((0, 0), (CONV_OFF, 0), (0, 0)))
    dww = jnp.pad(b_dw_w[0], ((0, HIST - CONV_W), (0, 0)))
    y_p, y_s = _conv_ffn_call(x, glu, norm_g[1, 2][None], ffn_w_in, ffn_w_out, (1, 1), b_w_out, (0,),
                              cache_pad, dww, b_dw_b[0][None], b_ln_g[0][None], b_ln_b[0][None],
                              final_norm_g[None], n_prompt_tiles, tiles_per_stream)

    n_p = batch * seq
    y_prompt = y_p.reshape(batch, seq, D_MODEL)
    y_sample = y_s.reshape(n_dec, dec_seq, D_MODEL)
    v_prompt = vst[None, :batch, TM - CHUNK_A:, :]
    v_sample = vst[batch].reshape(1, n_dec, dec_seq, D_A)
    g_prompt = glu[:n_p].reshape(1, batch, seq, D_CONV)[:, :, seq - (CONV_W - 1):, :]
    g_sample = glu[n_p:].reshape(1, n_dec, dec_seq, D_CONV)[:, :, dec_seq - (CONV_W - 1):, :]
    return (y_prompt, y_sample, v_prompt, v_sample, g_prompt, g_sample)
```

```python
import functools

import jax
import jax.numpy as jnp
from jax import lax
from jax.experimental import pallas as pl
from jax.experimental.pallas import tpu as pltpu

D_MODEL = 1024
D_FF = 2816
D_A = 2 * D_MODEL
CHUNK_A = 128
N_GROUPS_A = 8
D_GROUP_A = D_A // N_GROUPS_A
D_CONV = D_MODEL
CONV_W = 31
EPS = 1e-6

TM = 512
HM = TM // 2
RB = 32
LN_RB = 16
CONV_RB = 64
CW = 256
LANES = 128
SUBLANES = 8
HIST = 32
CONV_OFF = HIST - (CONV_W - 1)
VMEM_LIMIT = 56 * 1024 * 1024

_F32 = jnp.float32
_BF16 = jnp.bfloat16


def _const_spec(shape):
    nd = len(shape)
    return pl.BlockSpec(shape, lambda i: (0,) * nd, pipeline_mode=pl.Buffered(1))


def _blocks(start, n_rows, rb):
    return [slice(start + r * rb, start + (r + 1) * rb) for r in range(n_rows // rb)]


def _rms_rows(x, g_ref):
    ms = jnp.mean(x * x, axis=-1, keepdims=True)
    return x * lax.rsqrt(ms + EPS) * g_ref[...]


def _layer_norm_rows(x, g_ref, b_ref):
    mu = jnp.mean(x, axis=-1, keepdims=True)
    xc = x - mu
    var = jnp.mean(xc * xc, axis=-1, keepdims=True)
    return xc * lax.rsqrt(var + EPS) * g_ref[...] + b_ref[...]


def _tile_map(n_w, n_clip):
    return lambda i: (jnp.clip(i - n_w, 0, n_clip - 1), 0)


def _w_col_spec(lead, rows, cols, n_w, offset):
    none = (None,) * len(lead)
    return pl.BlockSpec(none + (rows, cols), lambda i: lead + (0, offset + jnp.minimum(i, n_w - 1)))


def _w_row_spec(lead, rows, cols, n_w):
    none = (None,) * len(lead)
    return pl.BlockSpec(none + (rows, cols), lambda i: lead + (jnp.minimum(i, n_w - 1), 0))


def _ffn_kernel(*refs, n_w, n_prompt_tiles, first, final):
    refs = list(refs)
    x_refs = [refs.pop(0) for _ in range(2 if first else 1)]
    g_ref, wa_ref, wb_ref, wo_ref = (refs.pop(0) for _ in range(4))
    gf_ref = refs.pop(0) if final else None
    o_refs = [refs.pop(0) for _ in range(2 if final else 1)]
    win_s, wout_s, h_ref, act_ref = (refs.pop(0) for _ in range(4))
    xin_ref = refs.pop(0) if first else x_refs[0]

    i = pl.program_id(0)
    is_prompt = i - n_w < n_prompt_tiles

    @pl.when(i < n_w)
    def _load_weights():
        win_s[i, :, :CW] = wa_ref[...].astype(_BF16)
        win_s[i, :, CW:] = wb_ref[...].astype(_BF16)
        wout_s[pl.ds(pl.multiple_of(i * CW, CW), CW), :] = wo_ref[...].astype(_BF16)

    @pl.when(i >= n_w)
    def _tile():
        for hh in range(TM // HM):
            for rows in _blocks(hh * HM, HM, RB):
                if first:
                    x = jnp.where(is_prompt, x_refs[0][rows, :], x_refs[1][rows, :])
                    xin_ref[rows, :] = x
                else:
                    x = xin_ref[rows, :]
                h_ref[rows, :] = _rms_rows(x, g_ref).astype(_BF16)
            rs = slice(hh * HM, (hh + 1) * HM)
            for c in range(D_FF // CW):
                ab = jnp.dot(h_ref[rs, :], win_s[c], preferred_element_type=_F32)
                a = ab[:, :CW]
                b = ab[:, CW:]
                act_ref[rs, c * CW:(c + 1) * CW] = (a * jax.nn.sigmoid(a) * b).astype(_BF16)
            y = jnp.dot(act_ref[rs, :], wout_s[...], preferred_element_type=_F32)
            if not final:
                o_refs[0][rs, :] = xin_ref[rs, :] + 0.5 * y
            else:
                o_refs[1][rs, :] = xin_ref[rs, :] + 0.5 * y
                for rows in _blocks(hh * HM, HM, RB):
                    o_refs[1][rows, :] = _rms_rows(o_refs[1][rows, :], gf_ref)
        if final:
            @pl.when(is_prompt)
            def _copy_out():
                o_refs[0][...] = o_refs[1][...]


def _ffn_call(xs, g, w_in, w_out, lead, n_prompt_tiles, gf=None):
    first = len(xs) == 2
    final = gf is not None
    n_w = D_FF // CW
    n_tiles = n_prompt_tiles + 1
    n_tok = n_tiles * TM
    tile_all = pl.BlockSpec((TM, D_MODEL), _tile_map(n_w, n_tiles))
    tile_prompt = pl.BlockSpec((TM, D_MODEL), _tile_map(n_w, n_prompt_tiles))
    tile_sample = pl.BlockSpec((TM, D_MODEL), lambda i: (0, 0))

    in_specs = [tile_prompt, tile_sample] if first else [tile_all]
    in_specs += [
        _const_spec((1, D_MODEL)),
        _w_col_spec(lead, D_MODEL, CW, n_w, 0),
        _w_col_spec(lead, D_MODEL, CW, n_w, n_w),
        _w_row_spec(lead, CW, D_MODEL, n_w),
    ]
    args = list(xs) + [g, w_in, w_in, w_out]
    if final:
        in_specs.append(_const_spec((1, D_MODEL)))
        args.append(gf)
        out_specs = [tile_prompt, tile_sample]
        out_shape = [jax.ShapeDtypeStruct((n_prompt_tiles * TM, D_MODEL), _F32),
                     jax.ShapeDtypeStruct((TM, D_MODEL), _F32)]
    else:
        out_specs = tile_all
        out_shape = jax.ShapeDtypeStruct((n_tok, D_MODEL), _F32)
    scratch = [
        pltpu.VMEM((n_w, D_MODEL, 2 * CW), _BF16),
        pltpu.VMEM((D_FF, D_MODEL), _BF16),
        pltpu.VMEM((TM, D_MODEL), _BF16),
        pltpu.VMEM((TM, D_FF), _BF16),
    ]
    if first:
        scratch.append(pltpu.VMEM((TM, D_MODEL), _F32))
    return pl.pallas_call(
        functools.partial(_ffn_kernel, n_w=n_w, n_prompt_tiles=n_prompt_tiles,
                          first=first, final=final),
        grid=(n_w + n_tiles,),
        in_specs=in_specs,
        out_specs=out_specs,
        out_shape=out_shape,
        scratch_shapes=scratch,
        compiler_params=pltpu.CompilerParams(
            dimension_semantics=("arbitrary",), vmem_limit_bytes=VMEM_LIMIT),
        name="ffn_first" if first else ("ffn_final" if final else "ffn"),
    )(*args)


def _gmlp_kernel(x_ref, g_ref, wi_ref, wo_ref, vg_ref, vb_ref, ws_ref, bs_ref,
                 o_ref, vst_ref, win_s, wout_s, h_ref, uv_ref, vnb_ref, wsb_ref, y_ref,
                 *, n_w, n_prompt_tiles):
    i = pl.program_id(0)
    t = i - n_w
    wo_rows = D_A // n_w
    ncol = 2 * D_A // n_w

    @pl.when(i < n_w)
    def _load_weights():
        win_s[i] = wi_ref[...].astype(_BF16)
        wout_s[pl.ds(pl.multiple_of(i * wo_rows, wo_rows), wo_rows), :] = wo_ref[...].astype(_BF16)

    @pl.when(i >= n_w)
    def _tile():
        row = lax.broadcasted_iota(jnp.int32, (CHUNK_A, CHUNK_A), 0)
        col = lax.broadcasted_iota(jnp.int32, (CHUNK_A, CHUNK_A), 1)
        same_stream = jnp.logical_or(t < n_prompt_tiles, (row // RB) == (col // RB))
        keep = jnp.logical_and(col <= row, same_stream)
        for gi in range(N_GROUPS_A):
            wsb_ref[gi] = jnp.where(keep, ws_ref[0, gi], 0.0).astype(_BF16)

        for hh in range(TM // HM):
            rs = slice(hh * HM, (hh + 1) * HM)
            for rows in _blocks(hh * HM, HM, RB):
                h_ref[rows, :] = _rms_rows(x_ref[rows, :], g_ref).astype(_BF16)

            for c in range(n_w):
                tt = jnp.dot(h_ref[rs, :], win_s[c], preferred_element_type=_F32)
                uv_ref[rs, c * ncol:(c + 1) * ncol] = 0.5 * tt * (1.0 + lax.erf(tt * (2.0 ** -0.5)))

            for rows in _blocks(hh * HM, HM, LN_RB):
                vn = _layer_norm_rows(uv_ref[rows, D_A:], vg_ref, vb_ref)
                vst_ref[0, rows, :] = vn
                vnb_ref[rows, :] = vn.astype(_BF16)

            for rows in _blocks(hh * HM, HM, CHUNK_A):
                for gi in range(N_GROUPS_A):
                    cols = slice(gi * D_GROUP_A, (gi + 1) * D_GROUP_A)
                    z = jnp.dot(wsb_ref[gi], vnb_ref[rows, cols], preferred_element_type=_F32)
                    z = z + bs_ref[0, gi]
                    y_ref[rows, cols] = (uv_ref[rows, cols] * z).astype(_BF16)

            out = jnp.dot(y_ref[rs, :], wout_s[...], preferred_element_type=_F32)
            o_ref[rs, :] = x_ref[rs, :] + out


def _gmlp_call(x, g, w_in, w_out, lead, vg, vb, ws2, bs2, n_prompt_tiles, tiles_per_stream):
    n_w = 16
    n_tiles = n_prompt_tiles + 1
    n_states = (n_tiles + tiles_per_stream - 1) // tiles_per_stream
    tile = pl.BlockSpec((TM, D_MODEL), _tile_map(n_w, n_tiles))
    group_map = lambda i: (jnp.maximum(i - n_w, 0) // n_prompt_tiles, 0, 0, 0)
    state_map = lambda i: (jnp.maximum(i - n_w, 0) // tiles_per_stream, 0, 0)
    return pl.pallas_call(
        functools.partial(_gmlp_kernel, n_w=n_w, n_prompt_tiles=n_prompt_tiles),
        grid=(n_w + n_tiles,),
        in_specs=[
            tile, _const_spec((1, D_MODEL)),
            _w_col_spec(lead, D_MODEL, 2 * D_A // n_w, n_w, 0),
            _w_row_spec(lead, D_A // n_w, D_MODEL, n_w),
            _const_spec((1, D_A)), _const_spec((1, D_A)),
            pl.BlockSpec((1, N_GROUPS_A, CHUNK_A, CHUNK_A), group_map),
            pl.BlockSpec((1, N_GROUPS_A, CHUNK_A, D_GROUP_A), group_map),
        ],
        out_specs=[tile, pl.BlockSpec((1, TM, D_A), state_map)],
        out_shape=[jax.ShapeDtypeStruct((n_tiles * TM, D_MODEL), _F32),
                   jax.ShapeDtypeStruct((n_states, TM, D_A), _F32)],
        scratch_shapes=[
            pltpu.VMEM((n_w, D_MODEL, 2 * D_A // n_w), _BF16),
            pltpu.VMEM((D_A, D_MODEL), _BF16),
            pltpu.VMEM((TM, D_MODEL), _BF16),
            pltpu.VMEM((TM, 2 * D_A), _F32),
            pltpu.VMEM((TM, D_A), _BF16),
            pltpu.VMEM((N_GROUPS_A, CHUNK_A, CHUNK_A), _BF16),
            pltpu.VMEM((TM, D_A), _BF16),
        ],
        compiler_params=pltpu.CompilerParams(
            dimension_semantics=("arbitrary",), vmem_limit_bytes=VMEM_LIMIT),
        name="gmlp",
    )(x, g, w_in, w_out, vg, vb, ws2, bs2)


def _conv_kernel(x_ref, g_ref, wa_ref, wb_ref, wo_ref, cache_ref, dww_ref, dwb_ref, lng_ref, lnb_ref,
                 o_ref, glu_ref, win_s, wout_s, h_ref, xp_ref, c_ref, cn_ref,
                 *, n_w, n_prompt_tiles, tiles_per_stream, n_dec, dec_seq):
    i = pl.program_id(0)
    t = i - n_w

    @pl.when(i < n_w)
    def _load_weights():
        win_s[i, :, :CW] = wa_ref[...].astype(_BF16)
        win_s[i, :, CW:] = wb_ref[...].astype(_BF16)
        wout_s[pl.ds(pl.multiple_of(i * CW, CW), CW), :] = wo_ref[...].astype(_BF16)

    def conv_block(src, dst, rb):
        for cb in range(D_CONV // LANES):
            cols = slice(cb * LANES, (cb + 1) * LANES)
            acc = jnp.broadcast_to(dwb_ref[:, cols], (rb, LANES))
            for k in range(CONV_W):
                o = k + CONV_OFF
                r, q = o % SUBLANES, o // SUBLANES
                acc = acc + xp_ref[r, pl.ds(src + q * SUBLANES, rb), cols] * dww_ref[k:k + 1, cols]
            c_ref[pl.ds(dst, rb), cols] = acc

    def conv_segment(copy, src, dst, rb):
        n_win = rb + HIST
        for cb in range(D_CONV // LANES):
            cols = slice(cb * LANES, (cb + 1) * LANES)
            win = xp_ref[copy, pl.ds(src, n_win), cols]
            acc = jnp.broadcast_to(dwb_ref[:, cols], (rb, LANES))
            for r in range(SUBLANES):
                offs = [o for o in range(r, HIST + 1, SUBLANES) if 0 <= o - CONV_OFF < CONV_W]
                shifted = pltpu.roll(win, n_win - r, 0) if r else win
                for o in offs:
                    k = o - CONV_OFF
                    acc = acc + shifted[o - r:o - r + rb, :] * dww_ref[k:k + 1, cols]
            c_ref[pl.ds(dst, rb), cols] = acc

    def rms_rows(rows):
        h_ref[rows, :] = _rms_rows(x_ref[rows, :], g_ref).astype(_BF16)

    def in_proj(rs, xp_row0):
        for c in range(D_CONV // CW):
            ag = jnp.dot(h_ref[rs, :], win_s[c], preferred_element_type=_F32)
            glu = ag[:, :CW] * jax.nn.sigmoid(ag[:, CW:])
            glu_ref[0, rs, c * CW:(c + 1) * CW] = glu
            if xp_row0 is not None:
                for r in range(SUBLANES):
                    xp_ref[r, xp_row0 - r:xp_row0 - r + rs.stop - rs.start, c * CW:(c + 1) * CW] = glu

    def ln_rows(rows):
        cn = _layer_norm_rows(c_ref[rows, :], lng_ref, lnb_ref)
        cn_ref[rows, :] = (cn * jax.nn.sigmoid(cn)).astype(_BF16)

    def out_proj(rs):
        out = jnp.dot(cn_ref[rs, :], wout_s[...], preferred_element_type=_F32)
        o_ref[rs, :] = x_ref[rs, :] + out

    @pl.when(jnp.logical_and(i >= n_w, t < n_prompt_tiles))
    def _prompt_tile():
        @pl.when(t % tiles_per_stream == 0)
        def _new_stream():
            for r in range(SUBLANES):
                xp_ref[r, 0:HIST, :] = jnp.zeros((HIST, D_CONV), _F32)
                xp_ref[r, TM:TM + HIST, :] = jnp.zeros((HIST, D_CONV), _F32)

        for hh in range(TM // HM):
            rs = slice(hh * HM, (hh + 1) * HM)
            for rows in _blocks(hh * HM, HM, RB):
                rms_rows(rows)
            in_proj(rs, HIST + hh * HM)
            for rows in _blocks(hh * HM, HM, CONV_RB):
                conv_block(rows.start, rows.start, CONV_RB)
            for rows in _blocks(hh * HM, HM, RB):
                ln_rows(rows)
            out_proj(rs)
        for r in range(SUBLANES):
            xp_ref[r, 0:HIST, :] = xp_ref[r, TM:TM + HIST, :]

    @pl.when(t >= n_prompt_tiles)
    def _sample_tile():
        def loop(n, rb, body):
            def step(r, carry):
                body(pl.ds(pl.multiple_of(r * rb, rb), rb))
                return carry
            lax.fori_loop(0, n, step, 0)

        loop(TM // RB, RB, rms_rows)
        in_proj(slice(0, TM), None)
        seg = HIST + dec_seq
        per_copy = TM // seg
        for s in range(n_dec):
            row0 = (s % per_copy) * seg
            xp_ref[s // per_copy, row0:row0 + HIST, :] = cache_ref[s]
            xp_ref[s // per_copy, row0 + HIST:row0 + seg, :] = (
                glu_ref[0, s * dec_seq:(s + 1) * dec_seq, :])

        def seg_body(s, carry):
            conv_segment(s // per_copy, pl.multiple_of((s % per_copy) * seg, seg),
                         pl.multiple_of(s * dec_seq, dec_seq), dec_seq)
            return carry
        lax.fori_loop(0, n_dec, seg_body, 0)
        loop(TM // RB, RB, ln_rows)
        out_proj(slice(0, TM))


def _conv_call(x, g, w_in, w_out, lead, cache_pad, dww, dwb, lng, lnb, n_prompt_tiles,
               tiles_per_stream):
    n_w = D_CONV // CW
    n_tiles = n_prompt_tiles + 1
    n_states = (n_tiles + tiles_per_stream - 1) // tiles_per_stream
    n_dec, hist, _ = cache_pad.shape
    dec_seq = TM // n_dec
    assert hist == HIST and dec_seq % SUBLANES == 0 and dec_seq >= CONV_W - 1
    assert n_dec * (HIST + dec_seq) <= SUBLANES * TM
    tile = pl.BlockSpec((TM, D_MODEL), _tile_map(n_w, n_tiles))
    state_map = lambda i: (jnp.maximum(i - n_w, 0) // tiles_per_stream, 0, 0)
    return pl.pallas_call(
        functools.partial(_conv_kernel, n_w=n_w, n_prompt_tiles=n_prompt_tiles,
                          tiles_per_stream=tiles_per_stream, n_dec=n_dec, dec_seq=dec_seq),
        grid=(n_w + n_tiles,),
        in_specs=[
            tile, _const_spec((1, D_MODEL)),
            _w_col_spec(lead, D_MODEL, CW, n_w, 0),
            _w_col_spec(lead, D_MODEL, CW, n_w, n_w),
            _w_row_spec(lead, CW, D_MODEL, n_w),
            _const_spec(cache_pad.shape), _const_spec(dww.shape), _const_spec((1, D_CONV)),
            _const_spec((1, D_CONV)), _const_spec((1, D_CONV)),
        ],
        out_specs=[tile, pl.BlockSpec((1, TM, D_CONV), state_map)],
        out_shape=[jax.ShapeDtypeStruct((n_tiles * TM, D_MODEL), _F32),
                   jax.ShapeDtypeStruct((n_states, TM, D_CONV), _F32)],
        scratch_shapes=[
            pltpu.VMEM((n_w, D_MODEL, 2 * CW), _BF16),
            pltpu.VMEM((D_CONV, D_MODEL), _BF16),
            pltpu.VMEM((TM, D_MODEL), _BF16),
            pltpu.VMEM((SUBLANES, HIST + TM, D_CONV), _F32),
            pltpu.VMEM((TM, D_CONV), _F32),
            pltpu.VMEM((TM, D_CONV), _BF16),
        ],
        compiler_params=pltpu.CompilerParams(
            dimension_semantics=("arbitrary",), vmem_limit_bytes=VMEM_LIMIT),
        name="convmod",
    )(x, g, w_in, w_in, w_out, cache_pad, dww, dwb, lng, lnb)


def kernel(x_prompt, x_sample, cache_conv, norm_g, ffn_w_in, ffn_w_out, a_w_in, a_v_ln_g, a_v_ln_b,
           a_w_s, a_b_s, a_w_out, b_w_in, b_dw_w, b_dw_b, b_ln_g, b_ln_b, b_w_out, final_norm_g):
    batch, seq, _ = x_prompt.shape
    n_dec, dec_seq, _ = x_sample.shape
    depth = norm_g.shape[0]
    assert seq % TM == 0 and n_dec * dec_seq == TM and dec_seq == RB and CHUNK_A % dec_seq == 0
    tiles_per_stream = seq // TM
    n_prompt_tiles = batch * tiles_per_stream

    xs = [x_prompt.reshape(batch * seq, D_MODEL), x_sample.reshape(n_dec * dec_seq, D_MODEL)]
    a_states, conv_states = [], []
    for i in range(depth):
        j = i // 2
        xs = [_ffn_call(xs, norm_g[i, 0][None], ffn_w_in, ffn_w_out, (i, 0), n_prompt_tiles)]
        if i % 2 == 0:
            rep = CHUNK_A // dec_seq
            ws2 = jnp.stack([a_w_s[j], jnp.tile(a_w_s[j][:, :dec_seq, :dec_seq], (1, rep, rep))])
            bs_p = jnp.broadcast_to(a_b_s[j][:, :, None], (N_GROUPS_A, CHUNK_A, D_GROUP_A))
            bs_s = jnp.broadcast_to(jnp.tile(a_b_s[j][:, :dec_seq], (1, rep))[:, :, None],
                                    (N_GROUPS_A, CHUNK_A, D_GROUP_A))
            x, vst = _gmlp_call(xs[0], norm_g[i, 1][None], a_w_in, a_w_out, (j,), a_v_ln_g[j][None],
                                a_v_ln_b[j][None], ws2, jnp.stack([bs_p, bs_s]),
                                n_prompt_tiles, tiles_per_stream)
            a_states.append(vst)
        else:
            cache_pad = jnp.pad(cache_conv[j], ((0, 0), (CONV_OFF, 0), (0, 0)))
            dww = jnp.pad(b_dw_w[j], ((0, HIST - CONV_W), (0, 0)))
            x, gst = _conv_call(xs[0], norm_g[i, 1][None], b_w_in, b_w_out, (j,), cache_pad, dww,
                                b_dw_b[j][None], b_ln_g[j][None], b_ln_b[j][None],
                                n_prompt_tiles, tiles_per_stream)
            conv_states.append(gst)
        last = i == depth - 1
        res = _ffn_call([x], norm_g[i, 2][None], ffn_w_in, ffn_w_out, (i, 1), n_prompt_tiles,
                        final_norm_g[None] if last else None)
        xs = res if last else [res]

    y_prompt = xs[0].reshape(batch, seq, D_MODEL)
    y_sample = xs[1].reshape(n_dec, dec_seq, D_MODEL)
    vst = jnp.stack(a_states)
    gst = jnp.stack(conv_states)
    v_prompt = vst[:, :batch, TM - CHUNK_A:, :]
    v_sample = vst[:, batch].reshape(-1, n_dec, dec_seq, D_A)
    g_prompt = gst[:, :batch, TM - (CONV_W - 1):, :]
    g_sample = gst[:, batch].reshape(-1, n_dec, dec_seq, D_CONV)[:, :, dec_seq - (CONV_W - 1):, :]
    return (y_prompt, y_sample, v_prompt, v_sample, g_prompt, g_sample)
```

```python
import functools

import jax
import jax.numpy as jnp
from jax import lax
from jax.experimental import pallas as pl
from jax.experimental.pallas import tpu as pltpu

D_MODEL = 1024
D_FF = 2816
D_A = 2 * D_MODEL
CHUNK_A = 128
N_GROUPS_A = 8
D_GROUP_A = D_A // N_GROUPS_A
D_CONV = D_MODEL
CONV_W = 31
EPS = 1e-6

TM = 512
TF = 1024
HM = TM // 2
RB = 32
LN_RB = 16
CONV_RB = 64
CW = 256
LANES = 128
SUBLANES = 8
HIST = 32
CONV_OFF = HIST - (CONV_W - 1)
VMEM_LIMIT = 56 * 1024 * 1024
FFN_VMEM_LIMIT = 58 * 1024 * 1024

_F32 = jnp.float32
_BF16 = jnp.bfloat16


def _const_spec(shape):
    nd = len(shape)
    return pl.BlockSpec(shape, lambda i: (0,) * nd, pipeline_mode=pl.Buffered(1))


def _blocks(start, n_rows, rb):
    return [slice(start + r * rb, start + (r + 1) * rb) for r in range(n_rows // rb)]


def _rms_rows(x, g_ref):
    ms = jnp.mean(x * x, axis=-1, keepdims=True)
    return x * lax.rsqrt(ms + EPS) * g_ref[...]


def _layer_norm_rows(x, g_ref, b_ref):
    mu = jnp.mean(x, axis=-1, keepdims=True)
    xc = x - mu
    var = jnp.mean(xc * xc, axis=-1, keepdims=True)
    return xc * lax.rsqrt(var + EPS) * g_ref[...] + b_ref[...]


def _tile_map(n_w, n_clip):
    return lambda i: (jnp.clip(i - n_w, 0, n_clip - 1), 0)


def _w_col_spec(lead, rows, cols, n_w, offset):
    none = (None,) * len(lead)
    return pl.BlockSpec(none + (rows, cols), lambda i: lead + (0, offset + jnp.minimum(i, n_w - 1)))


def _w_row_spec(lead, rows, cols, n_w):
    none = (None,) * len(lead)
    return pl.BlockSpec(none + (rows, cols), lambda i: lead + (jnp.minimum(i, n_w - 1), 0))


def _ffn_kernel(*refs, n_w, n_big, first, final):
    refs = list(refs)
    big_in = refs.pop(0)
    small_in = refs.pop(0) if first else big_in
    g_ref, wa_ref, wb_ref, wo_ref = (refs.pop(0) for _ in range(4))
    gf_ref = refs.pop(0) if final else None
    big_out = refs.pop(0)
    small_out = refs.pop(0) if final else big_out
    win_s, wout_s, h_ref, act_ref = refs

    i = pl.program_id(0)

    def rms_rows(src_ref, rows):
        h_ref[rows, :] = _rms_rows(src_ref[rows, :], g_ref).astype(_BF16)

    def final_norm(dst_ref, rs):
        for rows in _blocks(rs.start, rs.stop - rs.start, RB):
            dst_ref[rows, :] = _rms_rows(dst_ref[rows, :], gf_ref)

    def half(src_ref, dst_ref, rs):
        for rows in _blocks(rs.start, rs.stop - rs.start, RB):
            rms_rows(src_ref, rows)
        for c in range(n_w):
            ab = jnp.dot(h_ref[rs, :], win_s[c], preferred_element_type=_F32)
            a = ab[:, :CW]
            b = ab[:, CW:]
            act_ref[rs, c * CW:(c + 1) * CW] = (a * jax.nn.sigmoid(a) * b).astype(_BF16)
        y = jnp.dot(act_ref[rs, :], wout_s[...], preferred_element_type=_F32)
        dst_ref[rs, :] = src_ref[rs, :] + 0.5 * y
        if final:
            final_norm(dst_ref, rs)

    halves = [slice(hh * TM, (hh + 1) * TM) for hh in range(TF // TM)]

    @pl.when(i < n_w)
    def _weights_and_tile0():
        win_s[i, :, :CW] = wa_ref[...].astype(_BF16)
        win_s[i, :, CW:] = wb_ref[...].astype(_BF16)
        w_rows = pl.ds(pl.multiple_of(i * CW, CW), CW)
        wout_s[w_rows, :] = wo_ref[...].astype(_BF16)

        @pl.when(i == 0)
        def _start_tile0():
            for rows in _blocks(0, TF, RB):
                rms_rows(big_in, rows)
                big_out[rows, :] = big_in[rows, :]

        for rs in halves:
            ab = jnp.dot(h_ref[rs, :], win_s[i], preferred_element_type=_F32)
            a = ab[:, :CW]
            b = ab[:, CW:]
            act = (a * jax.nn.sigmoid(a) * b).astype(_BF16)
            big_out[rs, :] += 0.5 * jnp.dot(act, wout_s[w_rows, :], preferred_element_type=_F32)

        if final:
            @pl.when(i == n_w - 1)
            def _finish_tile0():
                for rs in halves:
                    final_norm(big_out, rs)

    @pl.when(jnp.logical_and(i >= n_w, i < n_w + n_big - 1))
    def _prompt_tile():
        for rs in halves:
            half(big_in, big_out, rs)

    @pl.when(i == n_w + n_big - 1)
    def _sample_tile():
        half(small_in, small_out, slice(0, TM))


def _ffn_call(xs, g, w_in, w_out, lead, n_prompt_tiles, gf=None):
    first = len(xs) == 2
    final = gf is not None
    n_w = D_FF // CW
    n_big = n_prompt_tiles * TM // TF
    assert n_prompt_tiles * TM % TF == 0
    n_tok = (n_prompt_tiles + 1) * TM

    def big_map(n_clip):
        return lambda i: (jnp.clip(i - (n_w - 1), 0, n_clip - 1), 0)

    big_all = pl.BlockSpec((TF, D_MODEL), big_map(n_big + 1))
    big_prompt = pl.BlockSpec((TF, D_MODEL), big_map(n_big))
    small = pl.BlockSpec((TM, D_MODEL), lambda i: (0, 0))

    in_specs = [big_prompt, small] if first else [big_all]
    in_specs += [
        _const_spec((1, D_MODEL)),
        _w_col_spec(lead, D_MODEL, CW, n_w, 0),
        _w_col_spec(lead, D_MODEL, CW, n_w, n_w),
        _w_row_spec(lead, CW, D_MODEL, n_w),
    ]
    args = list(xs) + [g, w_in, w_in, w_out]
    if final:
        in_specs.append(_const_spec((1, D_MODEL)))
        args.append(gf)
        out_specs = [big_prompt, small]
        out_shape = [jax.ShapeDtypeStruct((n_prompt_tiles * TM, D_MODEL), _F32),
                     jax.ShapeDtypeStruct((TM, D_MODEL), _F32)]
    else:
        out_specs = big_all
        out_shape = jax.ShapeDtypeStruct((n_tok, D_MODEL), _F32)
    return pl.pallas_call(
        functools.partial(_ffn_kernel, n_w=n_w, n_big=n_big, first=first, final=final),
        grid=(n_w + n_big,),
        in_specs=in_specs,
        out_specs=out_specs,
        out_shape=out_shape,
        scratch_shapes=[
            pltpu.VMEM((n_w, D_MODEL, 2 * CW), _BF16),
            pltpu.VMEM((D_FF, D_MODEL), _BF16),
            pltpu.VMEM((TF, D_MODEL), _BF16),
            pltpu.VMEM((TF, D_FF), _BF16),
        ],
        compiler_params=pltpu.CompilerParams(
            dimension_semantics=("arbitrary",), vmem_limit_bytes=FFN_VMEM_LIMIT),
        name="ffn_first" if first else ("ffn_final" if final else "ffn"),
    )(*args)


def _gmlp_kernel(x_ref, g_ref, wi_ref, wo_ref, vg_ref, vb_ref, ws_ref, bs_ref,
                 o_ref, vst_ref, win_s, wout_s, h_ref, uv_ref, vnb_ref, wsb_ref, y_ref,
                 *, n_w, n_prompt_tiles):
    i = pl.program_id(0)
    t = i - n_w
    wo_rows = D_A // n_w
    ncol = 2 * D_A // n_w

    @pl.when(i < n_w)
    def _load_weights():
        win_s[i] = wi_ref[...].astype(_BF16)
        wout_s[pl.ds(pl.multiple_of(i * wo_rows, wo_rows), wo_rows), :] = wo_ref[...].astype(_BF16)

    @pl.when(i >= n_w)
    def _tile():
        row = lax.broadcasted_iota(jnp.int32, (CHUNK_A, CHUNK_A), 0)
        col = lax.broadcasted_iota(jnp.int32, (CHUNK_A, CHUNK_A), 1)
        same_stream = jnp.logical_or(t < n_prompt_tiles, (row // RB) == (col // RB))
        keep = jnp.logical_and(col <= row, same_stream)
        for gi in range(N_GROUPS_A):
            wsb_ref[gi] = jnp.where(keep, ws_ref[0, gi], 0.0).astype(_BF16)

        for hh in range(TM // HM):
            rs = slice(hh * HM, (hh + 1) * HM)
            for rows in _blocks(hh * HM, HM, RB):
                h_ref[rows, :] = _rms_rows(x_ref[rows, :], g_ref).astype(_BF16)

            for c in range(n_w):
                tt = jnp.dot(h_ref[rs, :], win_s[c], preferred_element_type=_F32)
                uv_ref[rs, c * ncol:(c + 1) * ncol] = 0.5 * tt * (1.0 + lax.erf(tt * (2.0 ** -0.5)))

            for rows in _blocks(hh * HM, HM, LN_RB):
                vn = _layer_norm_rows(uv_ref[rows, D_A:], vg_ref, vb_ref)
                vst_ref[0, rows, :] = vn
                vnb_ref[rows, :] = vn.astype(_BF16)

            for rows in _blocks(hh * HM, HM, CHUNK_A):
                for gi in range(N_GROUPS_A):
                    cols = slice(gi * D_GROUP_A, (gi + 1) * D_GROUP_A)
                    z = jnp.dot(wsb_ref[gi], vnb_ref[rows, cols], preferred_element_type=_F32)
                    z = z + bs_ref[0, gi]
                    y_ref[rows, cols] = (uv_ref[rows, cols] * z).astype(_BF16)

            out = jnp.dot(y_ref[rs, :], wout_s[...], preferred_element_type=_F32)
            o_ref[rs, :] = x_ref[rs, :] + out


def _gmlp_call(x, g, w_in, w_out, lead, vg, vb, ws2, bs2, n_prompt_tiles, tiles_per_stream):
    n_w = 16
    n_tiles = n_prompt_tiles + 1
    n_states = (n_tiles + tiles_per_stream - 1) // tiles_per_stream
    tile = pl.BlockSpec((TM, D_MODEL), _tile_map(n_w, n_tiles))
    group_map = lambda i: (jnp.maximum(i - n_w, 0) // n_prompt_tiles, 0, 0, 0)
    state_map = lambda i: (jnp.maximum(i - n_w, 0) // tiles_per_stream, 0, 0)
    return pl.pallas_call(
        functools.partial(_gmlp_kernel, n_w=n_w, n_prompt_tiles=n_prompt_tiles),
        grid=(n_w + n_tiles,),
        in_specs=[
            tile, _const_spec((1, D_MODEL)),
            _w_col_spec(lead, D_MODEL, 2 * D_A // n_w, n_w, 0),
            _w_row_spec(lead, D_A // n_w, D_MODEL, n_w),
            _const_spec((1, D_A)), _const_spec((1, D_A)),
            pl.BlockSpec((1, N_GROUPS_A, CHUNK_A, CHUNK_A), group_map),
            pl.BlockSpec((1, N_GROUPS_A, CHUNK_A, D_GROUP_A), group_map),
        ],
        out_specs=[tile, pl.BlockSpec((1, TM, D_A), state_map)],
        out_shape=[jax.ShapeDtypeStruct((n_tiles * TM, D_MODEL), _F32),
                   jax.ShapeDtypeStruct((n_states, TM, D_A), _F32)],
        scratch_shapes=[
            pltpu.VMEM((n_w, D_MODEL, 2 * D_A // n_w), _BF16),
            pltpu.VMEM((D_A, D_MODEL), _BF16),
            pltpu.VMEM((TM, D_MODEL), _BF16),
            pltpu.VMEM((TM, 2 * D_A), _F32),
            pltpu.VMEM((TM, D_A), _BF16),
            pltpu.VMEM((N_GROUPS_A, CHUNK_A, CHUNK_A), _BF16),
            pltpu.VMEM((TM, D_A), _BF16),
        ],
        compiler_params=pltpu.CompilerParams(
            dimension_semantics=("arbitrary",), vmem_limit_bytes=VMEM_LIMIT),
        name="gmlp",
    )(x, g, w_in, w_out, vg, vb, ws2, bs2)


def _conv_kernel(x_ref, g_ref, wa_ref, wb_ref, wo_ref, cache_ref, dww_ref, dwb_ref, lng_ref, lnb_ref,
                 o_ref, glu_ref, win_s, wout_s, h_ref, xp_ref, c_ref, cn_ref,
                 *, n_w, n_prompt_tiles, tiles_per_stream, n_dec, dec_seq):
    i = pl.program_id(0)
    t = i - n_w

    @pl.when(i < n_w)
    def _load_weights():
        win_s[i, :, :CW] = wa_ref[...].astype(_BF16)
        win_s[i, :, CW:] = wb_ref[...].astype(_BF16)
        wout_s[pl.ds(pl.multiple_of(i * CW, CW), CW), :] = wo_ref[...].astype(_BF16)

    def conv_block(src, dst, rb):
        for cb in range(D_CONV // LANES):
            cols = slice(cb * LANES, (cb + 1) * LANES)
            acc = jnp.broadcast_to(dwb_ref[:, cols], (rb, LANES))
            for k in range(CONV_W):
                o = k + CONV_OFF
                r, q = o % SUBLANES, o // SUBLANES
                acc = acc + xp_ref[r, pl.ds(src + q * SUBLANES, rb), cols] * dww_ref[k:k + 1, cols]
            c_ref[pl.ds(dst, rb), cols] = acc

    def conv_segment(copy, src, dst, rb):
        n_win = rb + HIST
        for cb in range(D_CONV // LANES):
            cols = slice(cb * LANES, (cb + 1) * LANES)
            win = xp_ref[copy, pl.ds(src, n_win), cols]
            acc = jnp.broadcast_to(dwb_ref[:, cols], (rb, LANES))
            for r in range(SUBLANES):
                offs = [o for o in range(r, HIST + 1, SUBLANES) if 0 <= o - CONV_OFF < CONV_W]
                shifted = pltpu.roll(win, n_win - r, 0) if r else win
                for o in offs:
                    k = o - CONV_OFF
                    acc = acc + shifted[o - r:o - r + rb, :] * dww_ref[k:k + 1, cols]
            c_ref[pl.ds(dst, rb), cols] = acc

    def rms_rows(rows):
        h_ref[rows, :] = _rms_rows(x_ref[rows, :], g_ref).astype(_BF16)

    def in_proj(rs, xp_row0):
        for c in range(D_CONV // CW):
            ag = jnp.dot(h_ref[rs, :], win_s[c], preferred_element_type=_F32)
            glu = ag[:, :CW] * jax.nn.sigmoid(ag[:, CW:])
            glu_ref[0, rs, c * CW:(c + 1) * CW] = glu
            if xp_row0 is not None:
                for r in range(SUBLANES):
                    xp_ref[r, xp_row0 - r:xp_row0 - r + rs.stop - rs.start, c * CW:(c + 1) * CW] = glu

    def ln_rows(rows):
        cn = _layer_norm_rows(c_ref[rows, :], lng_ref, lnb_ref)
        cn_ref[rows, :] = (cn * jax.nn.sigmoid(cn)).astype(_BF16)

    def out_proj(rs):
        out = jnp.dot(cn_ref[rs, :], wout_s[...], preferred_element_type=_F32)
        o_ref[rs, :] = x_ref[rs, :] + out

    @pl.when(jnp.logical_and(i >= n_w, t < n_prompt_tiles))
    def _prompt_tile():
        @pl.when(t % tiles_per_stream == 0)
        def _new_stream():
            for r in range(SUBLANES):
                xp_ref[r, 0:HIST, :] = jnp.zeros((HIST, D_CONV), _F32)
                xp_ref[r, TM:TM + HIST, :] = jnp.zeros((HIST, D_CONV), _F32)

        for hh in range(TM // HM):
            rs = slice(hh * HM, (hh + 1) * HM)
            for rows in _blocks(hh * HM, HM, RB):
                rms_rows(rows)
            in_proj(rs, HIST + hh * HM)
            for rows in _blocks(hh * HM, HM, CONV_RB):
                conv_block(rows.start, rows.start, CONV_RB)
            for rows in _blocks(hh * HM, HM, RB):
                ln_rows(rows)
            out_proj(rs)
        for r in range(SUBLANES):
            xp_ref[r, 0:HIST, :] = xp_ref[r, TM:TM + HIST, :]

    @pl.when(t >= n_prompt_tiles)
    def _sample_tile():
        def loop(n, rb, body):
            def step(r, carry):
                body(pl.ds(pl.multiple_of(r * rb, rb), rb))
                return carry
            lax.fori_loop(0, n, step, 0)

        loop(TM // RB, RB, rms_rows)
        in_proj(slice(0, TM), None)
        seg = HIST + dec_seq
        per_copy = TM // seg
        for s in range(n_dec):
            row0 = (s % per_copy) * seg
            xp_ref[s // per_copy, row0:row0 + HIST, :] = cache_ref[s]
            xp_ref[s // per_copy, row0 + HIST:row0 + seg, :] = (
                glu_ref[0, s * dec_seq:(s + 1) * dec_seq, :])

        def seg_body(s, carry):
            conv_segment(s // per_copy, pl.multiple_of((s % per_copy) * seg, seg),
                         pl.multiple_of(s * dec_seq, dec_seq), dec_seq)
            return carry
        lax.fori_loop(0, n_dec, seg_body, 0)
        loop(TM // RB, RB, ln_rows)
        out_proj(slice(0, TM))


def _conv_call(x, g, w_in, w_out, lead, cache_pad, dww, dwb, lng, lnb, n_prompt_tiles,
               tiles_per_stream):
    n_w = D_CONV // CW
    n_tiles = n_prompt_tiles + 1
    n_states = (n_tiles + tiles_per_stream - 1) // tiles_per_stream
    n_dec, hist, _ = cache_pad.shape
    dec_seq = TM // n_dec
    assert hist == HIST and dec_seq % SUBLANES == 0 and dec_seq >= CONV_W - 1
    assert n_dec * (HIST + dec_seq) <= SUBLANES * TM
    tile = pl.BlockSpec((TM, D_MODEL), _tile_map(n_w, n_tiles))
    state_map = lambda i: (jnp.maximum(i - n_w, 0) // tiles_per_stream, 0, 0)
    return pl.pallas_call(
        functools.partial(_conv_kernel, n_w=n_w, n_prompt_tiles=n_prompt_tiles,
                          tiles_per_stream=tiles_per_stream, n_dec=n_dec, dec_seq=dec_seq),
        grid=(n_w + n_tiles,),
        in_specs=[
            tile, _const_spec((1, D_MODEL)),
            _w_col_spec(lead, D_MODEL, CW, n_w, 0),
            _w_col_spec(lead, D_MODEL, CW, n_w, n_w),
            _w_row_spec(lead, CW, D_MODEL, n_w),
            _const_spec(cache_pad.shape), _const_spec(dww.shape), _const_spec((1, D_CONV)),
            _const_spec((1, D_CONV)), _const_spec((1, D_CONV)),
        ],
        out_specs=[tile, pl.BlockSpec((1, TM, D_CONV), state_map)],
        out_shape=[jax.ShapeDtypeStruct((n_tiles * TM, D_MODEL), _F32),
                   jax.ShapeDtypeStruct((n_states, TM, D_CONV), _F32)],
        scratch_shapes=[
            pltpu.VMEM((n_w, D_MODEL, 2 * CW), _BF16),
            pltpu.VMEM((D_CONV, D_MODEL), _BF16),
            pltpu.VMEM((TM, D_MODEL), _BF16),
            pltpu.VMEM((SUBLANES, HIST + TM, D_CONV), _F32),
            pltpu.VMEM((TM, D_CONV), _F32),
            pltpu.VMEM((TM, D_CONV), _BF16),
        ],
        compiler_params=pltpu.CompilerParams(
            dimension_semantics=("arbitrary",), vmem_limit_bytes=VMEM_LIMIT),
        name="convmod",
    )(x, g, w_in, w_in, w_out, cache_pad, dww, dwb, lng, lnb)


def kernel(x_prompt, x_sample, cache_conv, norm_g, ffn_w_in, ffn_w_out, a_w_in, a_v_ln_g, a_v_ln_b,
           a_w_s, a_b_s, a_w_out, b_w_in, b_dw_w, b_dw_b, b_ln_g, b_ln_b, b_w_out, final_norm_g):
    batch, seq, _ = x_prompt.shape
    n_dec, dec_seq, _ = x_sample.shape
    depth = norm_g.shape[0]
    assert seq % TM == 0 and n_dec * dec_seq == TM and dec_seq == RB and CHUNK_A % dec_seq == 0
    tiles_per_stream = seq // TM
    n_prompt_tiles = batch * tiles_per_stream

    xs = [x_prompt.reshape(batch * seq, D_MODEL), x_sample.reshape(n_dec * dec_seq, D_MODEL)]
    a_states, conv_states = [], []
    for i in range(depth):
        j = i // 2
        xs = [_ffn_call(xs, norm_g[i, 0][None], ffn_w_in, ffn_w_out, (i, 0), n_prompt_tiles)]
        if i % 2 == 0:
            rep = CHUNK_A // dec_seq
            ws2 = jnp.stack([a_w_s[j], jnp.tile(a_w_s[j][:, :dec_seq, :dec_seq], (1, rep, rep))])
            bs_p = jnp.broadcast_to(a_b_s[j][:, :, None], (N_GROUPS_A, CHUNK_A, D_GROUP_A))
            bs_s = jnp.broadcast_to(jnp.tile(a_b_s[j][:, :dec_seq], (1, rep))[:, :, None],
                                    (N_GROUPS_A, CHUNK_A, D_GROUP_A))
            x, vst = _gmlp_call(xs[0], norm_g[i, 1][None], a_w_in, a_w_out, (j,), a_v_ln_g[j][None],
                                a_v_ln_b[j][None], ws2, jnp.stack([bs_p, bs_s]),
                                n_prompt_tiles, tiles_per_stream)
            a_states.append(vst)
        else:
            cache_pad = jnp.pad(cache_conv[j], ((0, 0), (CONV_OFF, 0), (0, 0)))
            dww = jnp.pad(b_dw_w[j], ((0, HIST - CONV_W), (0, 0)))
            x, gst = _conv_call(xs[0], norm_g[i, 1][None], b_w_in, b_w_out, (j,), cache_pad, dww,
                                b_dw_b[j][None], b_ln_g[j][None], b_ln_b[j][None],
                                n_prompt_tiles, tiles_per_stream)
            conv_states.append(gst)
        last = i == depth - 1
        res = _ffn_call([x], norm_g[i, 2][None], ffn_w_in, ffn_w_out, (i, 1), n_prompt_tiles,
                        final_norm_g[None] if last else None)
        xs = res if last else [res]

    y_prompt = xs[0].reshape(batch, seq, D_MODEL)
    y_sample = xs[1].reshape(n_dec, dec_seq, D_MODEL)
    vst = jnp.stack(a_states)
    gst = jnp.stack(conv_states)
    v_prompt = vst[:, :batch, TM - CHUNK_A:, :]
    v_sample = vst[:, batch].reshape(-1, n_dec, dec_seq, D_A)
    g_prompt = gst[:, :batch, TM - (CONV_W - 1):, :]
    g_sample = gst[:, batch].reshape(-1, n_dec, dec_seq, D_CONV)[:, :, dec_seq - (CONV_W - 1):, :]
    return (y_prompt, y_sample, v_prompt, v_sample, g_prompt, g_sample)
```

```python
import functools

import jax
import jax.numpy as jnp
from jax import lax
from jax.experimental import pallas as pl
from jax.experimental.pallas import tpu as pltpu

D_MODEL = 1024
D_FF = 2816
D_A = 2 * D_MODEL
CHUNK_A = 128
N_GROUPS_A = 8
D_GROUP_A = D_A // N_GROUPS_A
D_CONV = D_MODEL
CONV_W = 31
EPS = 1e-6

TM = 512
TF = 1024
HM = TM // 2
RB = 32
LN_RB = 16
CONV_RB = 64
CW = 256
LANES = 128
SUBLANES = 8
HIST = 32
CONV_OFF = HIST - (CONV_W - 1)
VMEM_LIMIT = 56 * 1024 * 1024
FFN_VMEM_LIMIT = 58 * 1024 * 1024

_F32 = jnp.float32
_BF16 = jnp.bfloat16


def _const_spec(shape):
    nd = len(shape)
    return pl.BlockSpec(shape, lambda i: (0,) * nd, pipeline_mode=pl.Buffered(1))


def _blocks(start, n_rows, rb):
    return [slice(start + r * rb, start + (r + 1) * rb) for r in range(n_rows // rb)]


def _rms_rows(x, g_ref):
    ms = jnp.mean(x * x, axis=-1, keepdims=True)
    return x * lax.rsqrt(ms + EPS) * g_ref[...]


def _layer_norm_rows(x, g_ref, b_ref):
    mu = jnp.mean(x, axis=-1, keepdims=True)
    xc = x - mu
    var = jnp.mean(xc * xc, axis=-1, keepdims=True)
    return xc * lax.rsqrt(var + EPS) * g_ref[...] + b_ref[...]


def _tile_map(n_w, n_clip):
    return lambda i: (jnp.clip(i - n_w, 0, n_clip - 1), 0)


def _w_col_spec(lead, rows, cols, n_w, offset):
    none = (None,) * len(lead)
    return pl.BlockSpec(none + (rows, cols), lambda i: lead + (0, offset + jnp.minimum(i, n_w - 1)))


def _w_row_spec(lead, rows, cols, n_w):
    none = (None,) * len(lead)
    return pl.BlockSpec(none + (rows, cols), lambda i: lead + (jnp.minimum(i, n_w - 1), 0))


def _ffn_kernel(*refs, n_w, n_big, first, final):
    refs = list(refs)
    big_in = refs.pop(0)
    small_in = refs.pop(0) if first else big_in
    g_ref, wa_ref, wb_ref, wo_ref = (refs.pop(0) for _ in range(4))
    gf_ref = refs.pop(0) if final else None
    big_out = refs.pop(0)
    small_out = refs.pop(0) if final else big_out
    win_s, wout_s, h_ref, act_ref = refs

    i = pl.program_id(0)

    def rms_rows(src_ref, rows):
        h_ref[rows, :] = _rms_rows(src_ref[rows, :], g_ref).astype(_BF16)

    def final_norm(dst_ref, rs):
        for rows in _blocks(rs.start, rs.stop - rs.start, RB):
            dst_ref[rows, :] = _rms_rows(dst_ref[rows, :], gf_ref)

    def half(src_ref, dst_ref, rs):
        for rows in _blocks(rs.start, rs.stop - rs.start, RB):
            rms_rows(src_ref, rows)
        for c in range(n_w):
            ab = jnp.dot(h_ref[rs, :], win_s[c], preferred_element_type=_F32)
            a = ab[:, :CW]
            b = ab[:, CW:]
            act_ref[rs, c * CW:(c + 1) * CW] = (a * jax.nn.sigmoid(a) * b).astype(_BF16)
        y = jnp.dot(act_ref[rs, :], wout_s[...], preferred_element_type=_F32)
        dst_ref[rs, :] = src_ref[rs, :] + 0.5 * y
        if final:
            final_norm(dst_ref, rs)

    halves = [slice(hh * TM, (hh + 1) * TM) for hh in range(TF // TM)]

    @pl.when(i < n_w)
    def _weights_and_tile0():
        win_s[i, :, :CW] = wa_ref[...].astype(_BF16)
        win_s[i, :, CW:] = wb_ref[...].astype(_BF16)
        w_rows = pl.ds(pl.multiple_of(i * CW, CW), CW)
        wout_s[w_rows, :] = wo_ref[...].astype(_BF16)

        @pl.when(i == 0)
        def _start_tile0():
            for rows in _blocks(0, TF, RB):
                rms_rows(big_in, rows)
                big_out[rows, :] = big_in[rows, :]

        for rs in halves:
            ab = jnp.dot(h_ref[rs, :], win_s[i], preferred_element_type=_F32)
            a = ab[:, :CW]
            b = ab[:, CW:]
            act = (a * jax.nn.sigmoid(a) * b).astype(_BF16)
            big_out[rs, :] += 0.5 * jnp.dot(act, wout_s[w_rows, :], preferred_element_type=_F32)

        if final:
            @pl.when(i == n_w - 1)
            def _finish_tile0():
                for rs in halves:
                    final_norm(big_out, rs)

    @pl.when(jnp.logical_and(i >= n_w, i < n_w + n_big - 1))
    def _prompt_tile():
        for rs in halves:
            half(big_in, big_out, rs)

    @pl.when(i == n_w + n_big - 1)
    def _sample_tile():
        half(small_in, small_out, slice(0, TM))


def _ffn_call(xs, g, w_in, w_out, lead, n_prompt_tiles, gf=None):
    first = len(xs) == 2
    final = gf is not None
    n_w = D_FF // CW
    n_big = n_prompt_tiles * TM // TF
    assert n_prompt_tiles * TM % TF == 0
    n_tok = (n_prompt_tiles + 1) * TM

    def big_map(n_clip):
        return lambda i: (jnp.clip(i - (n_w - 1), 0, n_clip - 1), 0)

    big_all = pl.BlockSpec((TF, D_MODEL), big_map(n_big + 1))
    big_prompt = pl.BlockSpec((TF, D_MODEL), big_map(n_big))
    small = pl.BlockSpec((TM, D_MODEL), lambda i: (0, 0))

    in_specs = [big_prompt, small] if first else [big_all]
    in_specs += [
        _const_spec((1, D_MODEL)),
        _w_col_spec(lead, D_MODEL, CW, n_w, 0),
        _w_col_spec(lead, D_MODEL, CW, n_w, n_w),
        _w_row_spec(lead, CW, D_MODEL, n_w),
    ]
    args = list(xs) + [g, w_in, w_in, w_out]
    if final:
        in_specs.append(_const_spec((1, D_MODEL)))
        args.append(gf)
        out_specs = [big_prompt, small]
        out_shape = [jax.ShapeDtypeStruct((n_prompt_tiles * TM, D_MODEL), _F32),
                     jax.ShapeDtypeStruct((TM, D_MODEL), _F32)]
    else:
        out_specs = big_all
        out_shape = jax.ShapeDtypeStruct((n_tok, D_MODEL), _F32)
    return pl.pallas_call(
        functools.partial(_ffn_kernel, n_w=n_w, n_big=n_big, first=first, final=final),
        grid=(n_w + n_big,),
        in_specs=in_specs,
        out_specs=out_specs,
        out_shape=out_shape,
        scratch_shapes=[
            pltpu.VMEM((n_w, D_MODEL, 2 * CW), _BF16),
            pltpu.VMEM((D_FF, D_MODEL), _BF16),
            pltpu.VMEM((TF, D_MODEL), _BF16),
            pltpu.VMEM((TF, D_FF), _BF16),
        ],
        compiler_params=pltpu.CompilerParams(
            dimension_semantics=("arbitrary",), vmem_limit_bytes=FFN_VMEM_LIMIT),
        name="ffn_first" if first else ("ffn_final" if final else "ffn"),
    )(*args)


def _gmlp_kernel(x_ref, g_ref, wi_ref, wo_ref, vg_ref, vb_ref, ws_ref, bs_ref,
                 o_ref, vst_ref, win_s, wout_s, h_ref, uv_ref, vnb_ref, wsb_ref, y_ref,
                 *, n_w, n_prompt_tiles):
    i = pl.program_id(0)
    t = i - n_w
    wo_rows = D_A // n_w
    ncol = 2 * D_A // n_w

    @pl.when(i < n_w)
    def _load_weights():
        win_s[i] = wi_ref[...].astype(_BF16)
        wout_s[pl.ds(pl.multiple_of(i * wo_rows, wo_rows), wo_rows), :] = wo_ref[...].astype(_BF16)

    @pl.when(i >= n_w)
    def _tile():
        row = lax.broadcasted_iota(jnp.int32, (CHUNK_A, CHUNK_A), 0)
        col = lax.broadcasted_iota(jnp.int32, (CHUNK_A, CHUNK_A), 1)
        same_stream = jnp.logical_or(t < n_prompt_tiles, (row // RB) == (col // RB))
        keep = jnp.logical_and(col <= row, same_stream)
        for gi in range(N_GROUPS_A):
            wsb_ref[gi] = jnp.where(keep, ws_ref[0, gi], 0.0).astype(_BF16)

        for rows in _blocks(0, TM, RB):
            h_ref[rows, :] = _rms_rows(x_ref[rows, :], g_ref).astype(_BF16)

        n_half = n_w // 2
        for c in list(range(n_half, n_w)) + list(range(n_half)):
            tt = jnp.dot(h_ref[...], win_s[c], preferred_element_type=_F32)
            uv_ref[:, c * ncol:(c + 1) * ncol] = 0.5 * tt * (1.0 + lax.erf(tt * (2.0 ** -0.5)))
            if c == n_w - 1:
                for rows in _blocks(0, TM, LN_RB):
                    vn = _layer_norm_rows(uv_ref[rows, D_A:], vg_ref, vb_ref)
                    vst_ref[0, rows, :] = vn
                    vnb_ref[rows, :] = vn.astype(_BF16)

        for hh in range(TM // HM):
            rs = slice(hh * HM, (hh + 1) * HM)
            for rows in _blocks(hh * HM, HM, CHUNK_A):
                for gi in range(N_GROUPS_A):
                    cols = slice(gi * D_GROUP_A, (gi + 1) * D_GROUP_A)
                    z = jnp.dot(wsb_ref[gi], vnb_ref[rows, cols], preferred_element_type=_F32)
                    z = z + bs_ref[0, gi]
                    y_ref[rows, cols] = (uv_ref[rows, cols] * z).astype(_BF16)

            out = jnp.dot(y_ref[rs, :], wout_s[...], preferred_element_type=_F32)
            o_ref[rs, :] = x_ref[rs, :] + out


def _gmlp_call(x, g, w_in, w_out, lead, vg, vb, ws2, bs2, n_prompt_tiles, tiles_per_stream):
    n_w = 8
    n_tiles = n_prompt_tiles + 1
    n_states = (n_tiles + tiles_per_stream - 1) // tiles_per_stream
    tile = pl.BlockSpec((TM, D_MODEL), _tile_map(n_w, n_tiles))
    group_map = lambda i: (jnp.maximum(i - n_w, 0) // n_prompt_tiles, 0, 0, 0)
    state_map = lambda i: (jnp.maximum(i - n_w, 0) // tiles_per_stream, 0, 0)
    return pl.pallas_call(
        functools.partial(_gmlp_kernel, n_w=n_w, n_prompt_tiles=n_prompt_tiles),
        grid=(n_w + n_tiles,),
        in_specs=[
            tile, _const_spec((1, D_MODEL)),
            _w_col_spec(lead, D_MODEL, 2 * D_A // n_w, n_w, 0),
            _w_row_spec(lead, D_A // n_w, D_MODEL, n_w),
            _const_spec((1, D_A)), _const_spec((1, D_A)),
            pl.BlockSpec((1, N_GROUPS_A, CHUNK_A, CHUNK_A), group_map),
            pl.BlockSpec((1, N_GROUPS_A, CHUNK_A, D_GROUP_A), group_map),
        ],
        out_specs=[tile, pl.BlockSpec((1, TM, D_A), state_map)],
        out_shape=[jax.ShapeDtypeStruct((n_tiles * TM, D_MODEL), _F32),
                   jax.ShapeDtypeStruct((n_states, TM, D_A), _F32)],
        scratch_shapes=[
            pltpu.VMEM((n_w, D_MODEL, 2 * D_A // n_w), _BF16),
            pltpu.VMEM((D_A, D_MODEL), _BF16),
            pltpu.VMEM((TM, D_MODEL), _BF16),
            pltpu.VMEM((TM, 2 * D_A), _F32),
            pltpu.VMEM((TM, D_A), _BF16),
            pltpu.VMEM((N_GROUPS_A, CHUNK_A, CHUNK_A), _BF16),
            pltpu.VMEM((TM, D_A), _BF16),
        ],
        compiler_params=pltpu.CompilerParams(
            dimension_semantics=("arbitrary",), vmem_limit_bytes=VMEM_LIMIT),
        name="gmlp",
    )(x, g, w_in, w_out, vg, vb, ws2, bs2)


def _conv_kernel(x_ref, g_ref, wa_ref, wb_ref, wo_ref, cache_ref, dww_ref, dwb_ref, lng_ref, lnb_ref,
                 o_ref, glu_ref, win_s, wout_s, h_ref, xp_ref, c_ref, cn_ref,
                 *, n_w, n_prompt_tiles, tiles_per_stream, n_dec, dec_seq):
    i = pl.program_id(0)
    t = i - n_w

    @pl.when(i < n_w)
    def _load_weights():
        win_s[i, :, :CW] = wa_ref[...].astype(_BF16)
        win_s[i, :, CW:] = wb_ref[...].astype(_BF16)
        wout_s[pl.ds(pl.multiple_of(i * CW, CW), CW), :] = wo_ref[...].astype(_BF16)

    def conv_block(src, dst, rb):
        for cb in range(D_CONV // LANES):
            cols = slice(cb * LANES, (cb + 1) * LANES)
            acc = jnp.broadcast_to(dwb_ref[:, cols], (rb, LANES))
            for k in range(CONV_W):
                o = k + CONV_OFF
                r, q = o % SUBLANES, o // SUBLANES
                acc = acc + xp_ref[r, pl.ds(src + q * SUBLANES, rb), cols] * dww_ref[k:k + 1, cols]
            c_ref[pl.ds(dst, rb), cols] = acc

    def conv_segment(copy, src, dst, rb):
        n_win = rb + HIST
        for cb in range(D_CONV // LANES):
            cols = slice(cb * LANES, (cb + 1) * LANES)
            win = xp_ref[copy, pl.ds(src, n_win), cols]
            acc = jnp.broadcast_to(dwb_ref[:, cols], (rb, LANES))
            for r in range(SUBLANES):
                offs = [o for o in range(r, HIST + 1, SUBLANES) if 0 <= o - CONV_OFF < CONV_W]
                shifted = pltpu.roll(win, n_win - r, 0) if r else win
                for o in offs:
                    k = o - CONV_OFF
                    acc = acc + shifted[o - r:o - r + rb, :] * dww_ref[k:k + 1, cols]
            c_ref[pl.ds(dst, rb), cols] = acc

    def rms_rows(rows):
        h_ref[rows, :] = _rms_rows(x_ref[rows, :], g_ref).astype(_BF16)

    def in_proj(rs, xp_row0):
        for c in range(D_CONV // CW):
            ag = jnp.dot(h_ref[rs, :], win_s[c], preferred_element_type=_F32)
            glu = ag[:, :CW] * jax.nn.sigmoid(ag[:, CW:])
            glu_ref[0, rs, c * CW:(c + 1) * CW] = glu
            if xp_row0 is not None:
                for r in range(SUBLANES):
                    xp_ref[r, xp_row0 - r:xp_row0 - r + rs.stop - rs.start, c * CW:(c + 1) * CW] = glu

    def ln_rows(rows):
        cn = _layer_norm_rows(c_ref[rows, :], lng_ref, lnb_ref)
        cn_ref[rows, :] = (cn * jax.nn.sigmoid(cn)).astype(_BF16)

    def out_proj(rs):
        out = jnp.dot(cn_ref[rs, :], wout_s[...], preferred_element_type=_F32)
        o_ref[rs, :] = x_ref[rs, :] + out

    @pl.when(jnp.logical_and(i >= n_w, t < n_prompt_tiles))
    def _prompt_tile():
        @pl.when(t % tiles_per_stream == 0)
        def _new_stream():
            for r in range(SUBLANES):
                xp_ref[r, 0:HIST, :] = jnp.zeros((HIST, D_CONV), _F32)
                xp_ref[r, TM:TM + HIST, :] = jnp.zeros((HIST, D_CONV), _F32)

        for hh in range(TM // HM):
            rs = slice(hh * HM, (hh + 1) * HM)
            for rows in _blocks(hh * HM, HM, RB):
                rms_rows(rows)
            in_proj(rs, HIST + hh * HM)
            for rows in _blocks(hh * HM, HM, CONV_RB):
                conv_block(rows.start, rows.start, CONV_RB)
            for rows in _blocks(hh * HM, HM, RB):
                ln_rows(rows)
            out_proj(rs)
        for r in range(SUBLANES):
            xp_ref[r, 0:HIST, :] = xp_ref[r, TM:TM + HIST, :]

    @pl.when(t >= n_prompt_tiles)
    def _sample_tile():
        def loop(n, rb, body):
            def step(r, carry):
                body(pl.ds(pl.multiple_of(r * rb, rb), rb))
                return carry
            lax.fori_loop(0, n, step, 0)

        loop(TM // RB, RB, rms_rows)
        in_proj(slice(0, TM), None)
        seg = HIST + dec_seq
        per_copy = TM // seg
        for s in range(n_dec):
            row0 = (s % per_copy) * seg
            xp_ref[s // per_copy, row0:row0 + HIST, :] = cache_ref[s]
            xp_ref[s // per_copy, row0 + HIST:row0 + seg, :] = (
                glu_ref[0, s * dec_seq:(s + 1) * dec_seq, :])

        def seg_body(s, carry):
            conv_segment(s // per_copy, pl.multiple_of((s % per_copy) * seg, seg),
                         pl.multiple_of(s * dec_seq, dec_seq), dec_seq)
            return carry
        lax.fori_loop(0, n_dec, seg_body, 0)
        loop(TM // RB, RB, ln_rows)
        out_proj(slice(0, TM))


def _conv_call(x, g, w_in, w_out, lead, cache_pad, dww, dwb, lng, lnb, n_prompt_tiles,
               tiles_per_stream):
    n_w = D_CONV // CW
    n_tiles = n_prompt_tiles + 1
    n_states = (n_tiles + tiles_per_stream - 1) // tiles_per_stream
    n_dec, hist, _ = cache_pad.shape
    dec_seq = TM // n_dec
    assert hist == HIST and dec_seq % SUBLANES == 0 and dec_seq >= CONV_W - 1
    assert n_dec * (HIST + dec_seq) <= SUBLANES * TM
    tile = pl.BlockSpec((TM, D_MODEL), _tile_map(n_w, n_tiles))
    state_map = lambda i: (jnp.maximum(i - n_w, 0) // tiles_per_stream, 0, 0)
    return pl.pallas_call(
        functools.partial(_conv_kernel, n_w=n_w, n_prompt_tiles=n_prompt_tiles,
                          tiles_per_stream=tiles_per_stream, n_dec=n_dec, dec_seq=dec_seq),
        grid=(n_w + n_tiles,),
        in_specs=[
            tile, _const_spec((1, D_MODEL)),
            _w_col_spec(lead, D_MODEL, CW, n_w, 0),
            _w_col_spec(lead, D_MODEL, CW, n_w, n_w),
            _w_row_spec(lead, CW, D_MODEL, n_w),
            _const_spec(cache_pad.shape), _const_spec(dww.shape), _const_spec((1, D_CONV)),
            _const_spec((1, D_CONV)), _const_spec((1, D_CONV)),
        ],
        out_specs=[tile, pl.BlockSpec((1, TM, D_CONV), state_map)],
        out_shape=[jax.ShapeDtypeStruct((n_tiles * TM, D_MODEL), _F32),
                   jax.ShapeDtypeStruct((n_states, TM, D_CONV), _F32)],
        scratch_shapes=[
            pltpu.VMEM((n_w, D_MODEL, 2 * CW), _BF16),
            pltpu.VMEM((D_CONV, D_MODEL), _BF16),
            pltpu.VMEM((TM, D_MODEL), _BF16),
            pltpu.VMEM((SUBLANES, HIST + TM, D_CONV), _F32),
            pltpu.VMEM((TM, D_CONV), _F32),
            pltpu.VMEM((TM, D_CONV), _BF16),
        ],
        compiler_params=pltpu.CompilerParams(
            dimension_semantics=("arbitrary",), vmem_limit_bytes=VMEM_LIMIT),
        name="convmod",
    )(x, g, w_in, w_in, w_out, cache_pad, dww, dwb, lng, lnb)


def kernel(x_prompt, x_sample, cache_conv, norm_g, ffn_w_in, ffn_w_out, a_w_in, a_v_ln_g, a_v_ln_b,
           a_w_s, a_b_s, a_w_out, b_w_in, b_dw_w, b_dw_b, b_ln_g, b_ln_b, b_w_out, final_norm_g):
    batch, seq, _ = x_prompt.shape
    n_dec, dec_seq, _ = x_sample.shape
    depth = norm_g.shape[0]
    assert seq % TM == 0 and n_dec * dec_seq == TM and dec_seq == RB and CHUNK_A % dec_seq == 0
    tiles_per_stream = seq // TM
    n_prompt_tiles = batch * tiles_per_stream

    xs = [x_prompt.reshape(batch * seq, D_MODEL), x_sample.reshape(n_dec * dec_seq, D_MODEL)]
    a_states, conv_states = [], []
    for i in range(depth):
        j = i // 2
        xs = [_ffn_call(xs, norm_g[i, 0][None], ffn_w_in, ffn_w_out, (i, 0), n_prompt_tiles)]
        if i % 2 == 0:
            rep = CHUNK_A // dec_seq
            ws2 = jnp.stack([a_w_s[j], jnp.tile(a_w_s[j][:, :dec_seq, :dec_seq], (1, rep, rep))])
            bs_p = jnp.broadcast_to(a_b_s[j][:, :, None], (N_GROUPS_A, CHUNK_A, D_GROUP_A))
            bs_s = jnp.broadcast_to(jnp.tile(a_b_s[j][:, :dec_seq], (1, rep))[:, :, None],
                                    (N_GROUPS_A, CHUNK_A, D_GROUP_A))
            x, vst = _gmlp_call(xs[0], norm_g[i, 1][None], a_w_in, a_w_out, (j,), a_v_ln_g[j][None],
                                a_v_ln_b[j][None], ws2, jnp.stack([bs_p, bs_s]),
                                n_prompt_tiles, tiles_per_stream)
            a_states.append(vst)
        else:
            cache_pad = jnp.pad(cache_conv[j], ((0, 0), (CONV_OFF, 0), (0, 0)))
            dww = jnp.pad(b_dw_w[j], ((0, HIST - CONV_W), (0, 0)))
            x, gst = _conv_call(xs[0], norm_g[i, 1][None], b_w_in, b_w_out, (j,), cache_pad, dww,
                                b_dw_b[j][None], b_ln_g[j][None], b_ln_b[j][None],
                                n_prompt_tiles, tiles_per_stream)
            conv_states.append(gst)
        last = i == depth - 1
        res = _ffn_call([x], norm_g[i, 2][None], ffn_w_in, ffn_w_out, (i, 1), n_prompt_tiles,
                        final_norm_g[None] if last else None)
        xs = res if last else [res]

    y_prompt = xs[0].reshape(batch, seq, D_MODEL)
    y_sample = xs[1].reshape(n_dec, dec_seq, D_MODEL)
    vst = jnp.stack(a_states)
    gst = jnp.stack(conv_states)
    v_prompt = vst[:, :batch, TM - CHUNK_A:, :]
    v_sample = vst[:, batch].reshape(-1, n_dec, dec_seq, D_A)
    g_prompt = gst[:, :batch, TM - (CONV_W - 1):, :]
    g_sample = gst[:, batch].reshape(-1, n_dec, dec_seq, D_CONV)[:, :, dec_seq - (CONV_W - 1):, :]
    return (y_prompt, y_sample, v_prompt, v_sample, g_prompt, g_sample)
```

```python
import functools

import jax
import jax.numpy as jnp
from jax import lax
from jax.experimental import pallas as pl
from jax.experimental.pallas import tpu as pltpu

D_MODEL = 1024
D_FF = 2816
D_A = 2 * D_MODEL
CHUNK_A = 128
N_GROUPS_A = 8
D_GROUP_A = D_A // N_GROUPS_A
D_CONV = D_MODEL
CONV_W = 31
EPS = 1e-6

TM = 512
TF = 1024
HM = TM // 2
RB = 32
LN_RB = 16
CONV_RB = 64
CW = 256
LANES = 128
SUBLANES = 8
HIST = 32
CONV_OFF = HIST - (CONV_W - 1)
VMEM_LIMIT = 56 * 1024 * 1024
FFN_VMEM_LIMIT = 58 * 1024 * 1024

_F32 = jnp.float32
_BF16 = jnp.bfloat16


def _const_spec(shape):
    nd = len(shape)
    return pl.BlockSpec(shape, lambda i: (0,) * nd, pipeline_mode=pl.Buffered(1))


def _blocks(start, n_rows, rb):
    return [slice(start + r * rb, start + (r + 1) * rb) for r in range(n_rows // rb)]


def _rms_rows(x, g_ref):
    ms = jnp.mean(x * x, axis=-1, keepdims=True)
    return x * lax.rsqrt(ms + EPS) * g_ref[...]


def _layer_norm_rows(x, g_ref, b_ref):
    mu = jnp.mean(x, axis=-1, keepdims=True)
    xc = x - mu
    var = jnp.mean(xc * xc, axis=-1, keepdims=True)
    return xc * lax.rsqrt(var + EPS) * g_ref[...] + b_ref[...]


def _tile_map(n_w, n_clip):
    return lambda i: (jnp.clip(i - n_w, 0, n_clip - 1), 0)


def _w_col_spec(lead, rows, cols, n_w, offset):
    none = (None,) * len(lead)
    return pl.BlockSpec(none + (rows, cols), lambda i: lead + (0, offset + jnp.minimum(i, n_w - 1)))


def _w_row_spec(lead, rows, cols, n_w):
    none = (None,) * len(lead)
    return pl.BlockSpec(none + (rows, cols), lambda i: lead + (jnp.minimum(i, n_w - 1), 0))


def _ffn_kernel(*refs, n_w, n_big, first, final):
    refs = list(refs)
    big_in = refs.pop(0)
    small_in = refs.pop(0) if first else big_in
    g_ref, wa_ref, wb_ref, wo_ref = (refs.pop(0) for _ in range(4))
    gf_ref = refs.pop(0) if final else None
    big_out = refs.pop(0)
    small_out = refs.pop(0) if final else big_out
    win_s, wout_s, h_ref, act_ref = refs

    i = pl.program_id(0)

    def rms_rows(src_ref, rows):
        h_ref[rows, :] = _rms_rows(src_ref[rows, :], g_ref).astype(_BF16)

    def final_norm(dst_ref, rs):
        for rows in _blocks(rs.start, rs.stop - rs.start, RB):
            dst_ref[rows, :] = _rms_rows(dst_ref[rows, :], gf_ref)

    def half(src_ref, dst_ref, rs):
        for rows in _blocks(rs.start, rs.stop - rs.start, RB):
            rms_rows(src_ref, rows)
        for c in range(n_w):
            ab = jnp.dot(h_ref[rs, :], win_s[c], preferred_element_type=_F32)
            a = ab[:, :CW]
            b = ab[:, CW:]
            act_ref[rs, c * CW:(c + 1) * CW] = (a * jax.nn.sigmoid(a) * b).astype(_BF16)
        y = jnp.dot(act_ref[rs, :], wout_s[...], preferred_element_type=_F32)
        dst_ref[rs, :] = src_ref[rs, :] + 0.5 * y
        if final:
            final_norm(dst_ref, rs)

    halves = [slice(hh * TM, (hh + 1) * TM) for hh in range(TF // TM)]

    @pl.when(i < n_w)
    def _weights_and_tile0():
        win_s[i, :, :CW] = wa_ref[...].astype(_BF16)
        win_s[i, :, CW:] = wb_ref[...].astype(_BF16)
        w_rows = pl.ds(pl.multiple_of(i * CW, CW), CW)
        wout_s[w_rows, :] = wo_ref[...].astype(_BF16)

        @pl.when(i == 0)
        def _start_tile0():
            for rows in _blocks(0, TF, RB):
                rms_rows(big_in, rows)
                big_out[rows, :] = big_in[rows, :]

        for rs in halves:
            ab = jnp.dot(h_ref[rs, :], win_s[i], preferred_element_type=_F32)
            a = ab[:, :CW]
            b = ab[:, CW:]
            act = (a * jax.nn.sigmoid(a) * b).astype(_BF16)
            big_out[rs, :] += 0.5 * jnp.dot(act, wout_s[w_rows, :], preferred_element_type=_F32)

        if final:
            @pl.when(i == n_w - 1)
            def _finish_tile0():
                for rs in halves:
                    final_norm(big_out, rs)

    @pl.when(jnp.logical_and(i >= n_w, i < n_w + n_big - 1))
    def _prompt_tile():
        for rs in halves:
            half(big_in, big_out, rs)

    @pl.when(i == n_w + n_big - 1)
    def _sample_tile():
        half(small_in, small_out, slice(0, TM))


def _ffn_call(xs, g, w_in, w_out, lead, n_prompt_tiles, gf=None):
    first = len(xs) == 2
    final = gf is not None
    n_w = D_FF // CW
    n_big = n_prompt_tiles * TM // TF
    assert n_prompt_tiles * TM % TF == 0
    n_tok = (n_prompt_tiles + 1) * TM

    def big_map(n_clip):
        return lambda i: (jnp.clip(i - (n_w - 1), 0, n_clip - 1), 0)

    big_all = pl.BlockSpec((TF, D_MODEL), big_map(n_big + 1))
    big_prompt = pl.BlockSpec((TF, D_MODEL), big_map(n_big))
    small = pl.BlockSpec((TM, D_MODEL), lambda i: (0, 0))

    in_specs = [big_prompt, small] if first else [big_all]
    in_specs += [
        _const_spec((1, D_MODEL)),
        _w_col_spec(lead, D_MODEL, CW, n_w, 0),
        _w_col_spec(lead, D_MODEL, CW, n_w, n_w),
        _w_row_spec(lead, CW, D_MODEL, n_w),
    ]
    args = list(xs) + [g, w_in, w_in, w_out]
    if final:
        in_specs.append(_const_spec((1, D_MODEL)))
        args.append(gf)
        out_specs = [big_prompt, small]
        out_shape = [jax.ShapeDtypeStruct((n_prompt_tiles * TM, D_MODEL), _F32),
                     jax.ShapeDtypeStruct((TM, D_MODEL), _F32)]
    else:
        out_specs = big_all
        out_shape = jax.ShapeDtypeStruct((n_tok, D_MODEL), _F32)
    return pl.pallas_call(
        functools.partial(_ffn_kernel, n_w=n_w, n_big=n_big, first=first, final=final),
        grid=(n_w + n_big,),
        in_specs=in_specs,
        out_specs=out_specs,
        out_shape=out_shape,
        scratch_shapes=[
            pltpu.VMEM((n_w, D_MODEL, 2 * CW), _BF16),
            pltpu.VMEM((D_FF, D_MODEL), _BF16),
            pltpu.VMEM((TF, D_MODEL), _BF16),
            pltpu.VMEM((TF, D_FF), _BF16),
        ],
        compiler_params=pltpu.CompilerParams(
            dimension_semantics=("arbitrary",), vmem_limit_bytes=FFN_VMEM_LIMIT),
        name="ffn_first" if first else ("ffn_final" if final else "ffn"),
    )(*args)


def _gmlp_kernel(x_ref, g_ref, wi_ref, wo_ref, vg_ref, vb_ref, ws_ref, bs_ref,
                 o_ref, vst_ref, win_s, wout_s, h_ref, uv_ref, vnb_ref, wsb_ref, y_ref,
                 *, n_w, n_prompt_tiles):
    i = pl.program_id(0)
    t = i - n_w
    wo_rows = D_A // n_w
    ncol = 2 * D_A // n_w

    @pl.when(i < n_w)
    def _load_weights():
        win_s[i] = wi_ref[...].astype(_BF16)
        wout_s[pl.ds(pl.multiple_of(i * wo_rows, wo_rows), wo_rows), :] = wo_ref[...].astype(_BF16)

    @pl.when(i >= n_w)
    def _tile():
        row = lax.broadcasted_iota(jnp.int32, (CHUNK_A, CHUNK_A), 0)
        col = lax.broadcasted_iota(jnp.int32, (CHUNK_A, CHUNK_A), 1)
        same_stream = jnp.logical_or(t < n_prompt_tiles, (row // RB) == (col // RB))
        keep = jnp.logical_and(col <= row, same_stream)
        for gi in range(N_GROUPS_A):
            wsb_ref[gi] = jnp.where(keep, ws_ref[0, gi], 0.0).astype(_BF16)

        for rows in _blocks(0, TM, RB):
            h_ref[rows, :] = _rms_rows(x_ref[rows, :], g_ref).astype(_BF16)

        n_half = n_w // 2
        for c in list(range(n_half, n_w)) + list(range(n_half)):
            tt = jnp.dot(h_ref[...], win_s[c], preferred_element_type=_F32)
            uv_ref[:, c * ncol:(c + 1) * ncol] = 0.5 * tt * (1.0 + lax.erf(tt * (2.0 ** -0.5)))
            if c == n_w - 1:
                for rows in _blocks(0, TM, LN_RB):
                    vn = _layer_norm_rows(uv_ref[rows, D_A:], vg_ref, vb_ref)
                    vst_ref[0, rows, :] = vn
                    vnb_ref[rows, :] = vn.astype(_BF16)

        for hh in range(TM // HM):
            rs = slice(hh * HM, (hh + 1) * HM)
            for rows in _blocks(hh * HM, HM, CHUNK_A):
                for gi in range(N_GROUPS_A):
                    cols = slice(gi * D_GROUP_A, (gi + 1) * D_GROUP_A)
                    z = jnp.dot(wsb_ref[gi], vnb_ref[rows, cols], preferred_element_type=_F32)
                    z = z + bs_ref[0, gi]
                    y_ref[rows, cols] = (uv_ref[rows, cols] * z).astype(_BF16)

            out = jnp.dot(y_ref[rs, :], wout_s[...], preferred_element_type=_F32)
            o_ref[rs, :] = x_ref[rs, :] + out


def _gmlp_call(x, g, w_in, w_out, lead, vg, vb, ws2, bs2, n_prompt_tiles, tiles_per_stream):
    n_w = 8
    n_tiles = n_prompt_tiles + 1
    n_states = (n_tiles + tiles_per_stream - 1) // tiles_per_stream
    tile = pl.BlockSpec((TM, D_MODEL), _tile_map(n_w, n_tiles))
    group_map = lambda i: (jnp.maximum(i - n_w, 0) // n_prompt_tiles, 0, 0, 0)
    state_map = lambda i: (jnp.maximum(i - n_w, 0) // tiles_per_stream, 0, 0)
    return pl.pallas_call(
        functools.partial(_gmlp_kernel, n_w=n_w, n_prompt_tiles=n_prompt_tiles),
        grid=(n_w + n_tiles,),
        in_specs=[
            tile, _const_spec((1, D_MODEL)),
            _w_col_spec(lead, D_MODEL, 2 * D_A // n_w, n_w, 0),
            _w_row_spec(lead, D_A // n_w, D_MODEL, n_w),
            _const_spec((1, D_A)), _const_spec((1, D_A)),
            pl.BlockSpec((1, N_GROUPS_A, CHUNK_A, CHUNK_A), group_map),
            pl.BlockSpec((1, N_GROUPS_A, CHUNK_A, D_GROUP_A), group_map),
        ],
        out_specs=[tile, pl.BlockSpec((1, TM, D_A), state_map)],
        out_shape=[jax.ShapeDtypeStruct((n_tiles * TM, D_MODEL), _F32),
                   jax.ShapeDtypeStruct((n_states, TM, D_A), _F32)],
        scratch_shapes=[
            pltpu.VMEM((n_w, D_MODEL, 2 * D_A // n_w), _BF16),
            pltpu.VMEM((D_A, D_MODEL), _BF16),
            pltpu.VMEM((TM, D_MODEL), _BF16),
            pltpu.VMEM((TM, 2 * D_A), _F32),
            pltpu.VMEM((TM, D_A), _BF16),
            pltpu.VMEM((N_GROUPS_A, CHUNK_A, CHUNK_A), _BF16),
            pltpu.VMEM((TM, D_A), _BF16),
        ],
        compiler_params=pltpu.CompilerParams(
            dimension_semantics=("arbitrary",), vmem_limit_bytes=VMEM_LIMIT),
        name="gmlp",
    )(x, g, w_in, w_out, vg, vb, ws2, bs2)


def _conv_kernel(x_ref, g_ref, wa_ref, wb_ref, wo_ref, cache_ref, dww_ref, dwb_ref, lng_ref, lnb_ref,
                 o_ref, glu_ref, win_s, wout_s, h_ref, xp_ref, c_ref, cn_ref,
                 *, n_w, n_prompt_tiles, tiles_per_stream, n_dec, dec_seq):
    i = pl.program_id(0)
    t = i - n_w

    @pl.when(i < n_w)
    def _load_weights():
        win_s[i, :, :CW] = wa_ref[...].astype(_BF16)
        win_s[i, :, CW:] = wb_ref[...].astype(_BF16)
        wout_s[pl.ds(pl.multiple_of(i * CW, CW), CW), :] = wo_ref[...].astype(_BF16)

    def conv_block(src, dst, rb):
        for cb in range(D_CONV // LANES):
            cols = slice(cb * LANES, (cb + 1) * LANES)
            acc = jnp.broadcast_to(dwb_ref[:, cols], (rb, LANES))
            for k in range(CONV_W):
                o = k + CONV_OFF
                r, q = o % SUBLANES, o // SUBLANES
                acc = acc + xp_ref[r, pl.ds(src + q * SUBLANES, rb), cols] * dww_ref[k:k + 1, cols]
            c_ref[pl.ds(dst, rb), cols] = acc

    def conv_segment(copy, src, dst, rb):
        n_win = rb + HIST
        for cb in range(D_CONV // LANES):
            cols = slice(cb * LANES, (cb + 1) * LANES)
            win = xp_ref[copy, pl.ds(src, n_win), cols]
            acc = jnp.broadcast_to(dwb_ref[:, cols], (rb, LANES))
            for r in range(SUBLANES):
                offs = [o for o in range(r, HIST + 1, SUBLANES) if 0 <= o - CONV_OFF < CONV_W]
                shifted = pltpu.roll(win, n_win - r, 0) if r else win
                for o in offs:
                    k = o - CONV_OFF
                    acc = acc + shifted[o - r:o - r + rb, :] * dww_ref[k:k + 1, cols]
            c_ref[pl.ds(dst, rb), cols] = acc

    def rms_rows(rows):
        h_ref[rows, :] = _rms_rows(x_ref[rows, :], g_ref).astype(_BF16)

    def in_proj(rs, xp_row0):
        for c in range(D_CONV // CW):
            ag = jnp.dot(h_ref[rs, :], win_s[c], preferred_element_type=_F32)
            glu = ag[:, :CW] * jax.nn.sigmoid(ag[:, CW:])
            glu_ref[0, rs, c * CW:(c + 1) * CW] = glu
            if xp_row0 is not None:
                for r in range(SUBLANES):
                    xp_ref[r, xp_row0 - r:xp_row0 - r + rs.stop - rs.start, c * CW:(c + 1) * CW] = glu

    def ln_rows(rows):
        cn = _layer_norm_rows(c_ref[rows, :], lng_ref, lnb_ref)
        cn_ref[rows, :] = (cn * jax.nn.sigmoid(cn)).astype(_BF16)

    def out_proj(rs):
        out = jnp.dot(cn_ref[rs, :], wout_s[...], preferred_element_type=_F32)
        o_ref[rs, :] = x_ref[rs, :] + out

    @pl.when(jnp.logical_and(i >= n_w, t < n_prompt_tiles))
    def _prompt_tile():
        @pl.when(t % tiles_per_stream == 0)
        def _new_stream():
            for r in range(SUBLANES):
                xp_ref[r, 0:HIST, :] = jnp.zeros((HIST, D_CONV), _F32)
                xp_ref[r, TM:TM + HIST, :] = jnp.zeros((HIST, D_CONV), _F32)

        for hh in range(TM // HM):
            rs = slice(hh * HM, (hh + 1) * HM)
            for rows in _blocks(hh * HM, HM, RB):
                rms_rows(rows)
            in_proj(rs, HIST + hh * HM)
            for rows in _blocks(hh * HM, HM, CONV_RB):
                conv_block(rows.start, rows.start, CONV_RB)
            for rows in _blocks(hh * HM, HM, RB):
                ln_rows(rows)
            out_proj(rs)
        for r in range(SUBLANES):
            xp_ref[r, 0:HIST, :] = xp_ref[r, TM:TM + HIST, :]

    @pl.when(t >= n_prompt_tiles)
    def _sample_tile():
        def loop(n, rb, body):
            def step(r, carry):
                body(pl.ds(pl.multiple_of(r * rb, rb), rb))
                return carry
            lax.fori_loop(0, n, step, 0)

        loop(TM // RB, RB, rms_rows)
        in_proj(slice(0, TM), None)
        seg = HIST + dec_seq
        per_copy = TM // seg
        for s in range(n_dec):
            row0 = (s % per_copy) * seg
            xp_ref[s // per_copy, row0:row0 + HIST, :] = cache_ref[s]
            xp_ref[s // per_copy, row0 + HIST:row0 + seg, :] = (
                glu_ref[0, s * dec_seq:(s + 1) * dec_seq, :])

        def seg_body(s, carry):
            conv_segment(s // per_copy, pl.multiple_of((s % per_copy) * seg, seg),
                         pl.multiple_of(s * dec_seq, dec_seq), dec_seq)
            return carry
        lax.fori_loop(0, n_dec, seg_body, 0)
        loop(TM // RB, RB, ln_rows)
        out_proj(slice(0, TM))


def _conv_call(x, g, w_in, w_out, lead, cache_pad, dww, dwb, lng, lnb, n_prompt_tiles,
               tiles_per_stream):
    n_w = D_CONV // CW
    n_tiles = n_prompt_tiles + 1
    n_states = (n_tiles + tiles_per_stream - 1) // tiles_per_stream
    n_dec, hist, _ = cache_pad.shape
    dec_seq = TM // n_dec
    assert hist == HIST and dec_seq % SUBLANES == 0 and dec_seq >= CONV_W - 1
    assert n_dec * (HIST + dec_seq) <= SUBLANES * TM
    tile = pl.BlockSpec((TM, D_MODEL), _tile_map(n_w, n_tiles))
    state_map = lambda i: (jnp.maximum(i - n_w, 0) // tiles_per_stream, 0, 0)
    return pl.pallas_call(
        functools.partial(_conv_kernel, n_w=n_w, n_prompt_tiles=n_prompt_tiles,
                          tiles_per_stream=tiles_per_stream, n_dec=n_dec, dec_seq=dec_seq),
        grid=(n_w + n_tiles,),
        in_specs=[
            tile, _const_spec((1, D_MODEL)),
            _w_col_spec(lead, D_MODEL, CW, n_w, 0),
            _w_col_spec(lead, D_MODEL, CW, n_w, n_w),
            _w_row_spec(lead, CW, D_MODEL, n_w),
            _const_spec(cache_pad.shape), _const_spec(dww.shape), _const_spec((1, D_CONV)),
            _const_spec((1, D_CONV)), _const_spec((1, D_CONV)),
        ],
        out_specs=[tile, pl.BlockSpec((1, TM, D_CONV), state_map)],
        out_shape=[jax.ShapeDtypeStruct((n_tiles * TM, D_MODEL), _F32),
                   jax.ShapeDtypeStruct((n_states, TM, D_CONV), _F32)],
        scratch_shapes=[
            pltpu.VMEM((n_w, D_MODEL, 2 * CW), _BF16),
            pltpu.VMEM((D_CONV, D_MODEL), _BF16),
            pltpu.VMEM((TM, D_MODEL), _BF16),
            pltpu.VMEM((SUBLANES, HIST + TM + SUBLANES, D_CONV), _F32),
            pltpu.VMEM((TM, D_CONV), _F32),
            pltpu.VMEM((TM, D_CONV), _BF16),
        ],
        compiler_params=pltpu.CompilerParams(
            dimension_semantics=("arbitrary",), vmem_limit_bytes=VMEM_LIMIT),
        name="convmod",
    )(x, g, w_in, w_in, w_out, cache_pad, dww, dwb, lng, lnb)


def kernel(x_prompt, x_sample, cache_conv, norm_g, ffn_w_in, ffn_w_out, a_w_in, a_v_ln_g, a_v_ln_b,
           a_w_s, a_b_s, a_w_out, b_w_in, b_dw_w, b_dw_b, b_ln_g, b_ln_b, b_w_out, final_norm_g):
    batch, seq, _ = x_prompt.shape
    n_dec, dec_seq, _ = x_sample.shape
    depth = norm_g.shape[0]
    assert seq % TM == 0 and n_dec * dec_seq == TM and dec_seq == RB and CHUNK_A % dec_seq == 0
    tiles_per_stream = seq // TM
    n_prompt_tiles = batch * tiles_per_stream

    xs = [x_prompt.reshape(batch * seq, D_MODEL), x_sample.reshape(n_dec * dec_seq, D_MODEL)]
    a_states, conv_states = [], []
    for i in range(depth):
        j = i // 2
        xs = [_ffn_call(xs, norm_g[i, 0][None], ffn_w_in, ffn_w_out, (i, 0), n_prompt_tiles)]
        if i % 2 == 0:
            rep = CHUNK_A // dec_seq
            ws2 = jnp.stack([a_w_s[j], jnp.tile(a_w_s[j][:, :dec_seq, :dec_seq], (1, rep, rep))])
            bs_p = jnp.broadcast_to(a_b_s[j][:, :, None], (N_GROUPS_A, CHUNK_A, D_GROUP_A))
            bs_s = jnp.broadcast_to(jnp.tile(a_b_s[j][:, :dec_seq], (1, rep))[:, :, None],
                                    (N_GROUPS_A, CHUNK_A, D_GROUP_A))
            x, vst = _gmlp_call(xs[0], norm_g[i, 1][None], a_w_in, a_w_out, (j,), a_v_ln_g[j][None],
                                a_v_ln_b[j][None], ws2, jnp.stack([bs_p, bs_s]),
                                n_prompt_tiles, tiles_per_stream)
            a_states.append(vst)
        else:
            cache_pad = jnp.pad(cache_conv[j], ((0, 0), (CONV_OFF, 0), (0, 0)))
            dww = jnp.pad(b_dw_w[j], ((0, HIST - CONV_W), (0, 0)))
            x, gst = _conv_call(xs[0], norm_g[i, 1][None], b_w_in, b_w_out, (j,), cache_pad, dww,
                                b_dw_b[j][None], b_ln_g[j][None], b_ln_b[j][None],
                                n_prompt_tiles, tiles_per_stream)
            conv_states.append(gst)
        last = i == depth - 1
        res = _ffn_call([x], norm_g[i, 2][None], ffn_w_in, ffn_w_out, (i, 1), n_prompt_tiles,
                        final_norm_g[None] if last else None)
        xs = res if last else [res]

    y_prompt = xs[0].reshape(batch, seq, D_MODEL)
    y_sample = xs[1].reshape(n_dec, dec_seq, D_MODEL)
    vst = jnp.stack(a_states)
    gst = jnp.stack(conv_states)
    v_prompt = vst[:, :batch, TM - CHUNK_A:, :]
    v_sample = vst[:, batch].reshape(-1, n_dec, dec_seq, D_A)
    g_prompt = gst[:, :batch, TM - (CONV_W - 1):, :]
    g_sample = gst[:, batch].reshape(-1, n_dec, dec_seq, D_CONV)[:, :, dec_seq - (CONV_W - 1):, :]
    return (y_prompt, y_sample, v_prompt, v_sample, g_prompt, g_sample)
```

```python
import functools

import jax
import jax.numpy as jnp
from jax import lax
from jax.experimental import pallas as pl
from jax.experimental.pallas import tpu as pltpu

D_MODEL = 1024
D_FF = 2816
D_A = 2 * D_MODEL
CHUNK_A = 128
N_GROUPS_A = 8
D_GROUP_A = D_A // N_GROUPS_A
D_CONV = D_MODEL
CONV_W = 31
EPS = 1e-6

TM = 512
TF = 1024
HM = TM // 2
RB = 32
LN_RB = 16
CONV_RB = 64
CW = 256
LANES = 128
SUBLANES = 8
HIST = 32
CONV_OFF = HIST - (CONV_W - 1)
VMEM_LIMIT = 56 * 1024 * 1024
FFN_VMEM_LIMIT = 58 * 1024 * 1024

_F32 = jnp.float32
_BF16 = jnp.bfloat16


def _const_spec(shape):
    nd = len(shape)
    return pl.BlockSpec(shape, lambda i: (0,) * nd, pipeline_mode=pl.Buffered(1))


def _blocks(start, n_rows, rb):
    return [slice(start + r * rb, start + (r + 1) * rb) for r in range(n_rows // rb)]


def _rms_rows(x, g_ref):
    ms = jnp.mean(x * x, axis=-1, keepdims=True)
    return x * lax.rsqrt(ms + EPS) * g_ref[...]


def _layer_norm_rows(x, g_ref, b_ref):
    mu = jnp.mean(x, axis=-1, keepdims=True)
    xc = x - mu
    var = jnp.mean(xc * xc, axis=-1, keepdims=True)
    return xc * lax.rsqrt(var + EPS) * g_ref[...] + b_ref[...]


def _tile_map(n_w, n_clip):
    return lambda i: (jnp.clip(i - n_w, 0, n_clip - 1), 0)


def _w_col_spec(lead, rows, cols, n_w, offset):
    none = (None,) * len(lead)
    return pl.BlockSpec(none + (rows, cols), lambda i: lead + (0, offset + jnp.minimum(i, n_w - 1)))


def _w_row_spec(lead, rows, cols, n_w):
    none = (None,) * len(lead)
    return pl.BlockSpec(none + (rows, cols), lambda i: lead + (jnp.minimum(i, n_w - 1), 0))


def _ffn_kernel(*refs, n_w, n_big, first, final):
    refs = list(refs)
    big_in = refs.pop(0)
    small_in = refs.pop(0) if first else big_in
    g_ref, wa_ref, wb_ref, wo_ref = (refs.pop(0) for _ in range(4))
    gf_ref = refs.pop(0) if final else None
    big_out = refs.pop(0)
    small_out = refs.pop(0) if final else big_out
    win_s, wout_s, h_ref, act_ref = refs

    i = pl.program_id(0)

    def rms_rows(src_ref, rows):
        h_ref[rows, :] = _rms_rows(src_ref[rows, :], g_ref).astype(_BF16)

    def final_norm(dst_ref, rs):
        for rows in _blocks(rs.start, rs.stop - rs.start, RB):
            dst_ref[rows, :] = _rms_rows(dst_ref[rows, :], gf_ref)

    def half(src_ref, dst_ref, rs):
        for rows in _blocks(rs.start, rs.stop - rs.start, RB):
            rms_rows(src_ref, rows)
        for c in range(n_w):
            ab = jnp.dot(h_ref[rs, :], win_s[c], preferred_element_type=_F32)
            a = ab[:, :CW]
            b = ab[:, CW:]
            act_ref[rs, c * CW:(c + 1) * CW] = (a * jax.nn.sigmoid(a) * b).astype(_BF16)
        y = jnp.dot(act_ref[rs, :], wout_s[...], preferred_element_type=_F32)
        dst_ref[rs, :] = src_ref[rs, :] + 0.5 * y
        if final:
            final_norm(dst_ref, rs)

    halves = [slice(hh * TM, (hh + 1) * TM) for hh in range(TF // TM)]

    @pl.when(i < n_w)
    def _weights_and_tile0():
        win_s[i, :, :CW] = wa_ref[...].astype(_BF16)
        win_s[i, :, CW:] = wb_ref[...].astype(_BF16)
        w_rows = pl.ds(pl.multiple_of(i * CW, CW), CW)
        wout_s[w_rows, :] = wo_ref[...].astype(_BF16)

        @pl.when(i == 0)
        def _start_tile0():
            for rows in _blocks(0, TF, RB):
                rms_rows(big_in, rows)
                big_out[rows, :] = big_in[rows, :]

        for rs in halves:
            ab = jnp.dot(h_ref[rs, :], win_s[i], preferred_element_type=_F32)
            a = ab[:, :CW]
            b = ab[:, CW:]
            act = (a * jax.nn.sigmoid(a) * b).astype(_BF16)
            big_out[rs, :] += 0.5 * jnp.dot(act, wout_s[w_rows, :], preferred_element_type=_F32)

        if final:
            @pl.when(i == n_w - 1)
            def _finish_tile0():
                for rs in halves:
                    final_norm(big_out, rs)

    @pl.when(jnp.logical_and(i >= n_w, i < n_w + n_big - 1))
    def _prompt_tile():
        for rs in halves:
            half(big_in, big_out, rs)

    @pl.when(i == n_w + n_big - 1)
    def _sample_tile():
        half(small_in, small_out, slice(0, TM))


def _ffn_call(xs, g, w_in, w_out, lead, n_prompt_tiles, gf=None):
    first = len(xs) == 2
    final = gf is not None
    n_w = D_FF // CW
    n_big = n_prompt_tiles * TM // TF
    assert n_prompt_tiles * TM % TF == 0
    n_tok = (n_prompt_tiles + 1) * TM

    def big_map(n_clip):
        return lambda i: (jnp.clip(i - (n_w - 1), 0, n_clip - 1), 0)

    big_all = pl.BlockSpec((TF, D_MODEL), big_map(n_big + 1))
    big_prompt = pl.BlockSpec((TF, D_MODEL), big_map(n_big))
    small = pl.BlockSpec((TM, D_MODEL), lambda i: (0, 0))

    in_specs = [big_prompt, small] if first else [big_all]
    in_specs += [
        _const_spec((1, D_MODEL)),
        _w_col_spec(lead, D_MODEL, CW, n_w, 0),
        _w_col_spec(lead, D_MODEL, CW, n_w, n_w),
        _w_row_spec(lead, CW, D_MODEL, n_w),
    ]
    args = list(xs) + [g, w_in, w_in, w_out]
    if final:
        in_specs.append(_const_spec((1, D_MODEL)))
        args.append(gf)
        out_specs = [big_prompt, small]
        out_shape = [jax.ShapeDtypeStruct((n_prompt_tiles * TM, D_MODEL), _F32),
                     jax.ShapeDtypeStruct((TM, D_MODEL), _F32)]
    else:
        out_specs = big_all
        out_shape = jax.ShapeDtypeStruct((n_tok, D_MODEL), _F32)
    return pl.pallas_call(
        functools.partial(_ffn_kernel, n_w=n_w, n_big=n_big, first=first, final=final),
        grid=(n_w + n_big,),
        in_specs=in_specs,
        out_specs=out_specs,
        out_shape=out_shape,
        scratch_shapes=[
            pltpu.VMEM((n_w, D_MODEL, 2 * CW), _BF16),
            pltpu.VMEM((D_FF, D_MODEL), _BF16),
            pltpu.VMEM((TF, D_MODEL), _BF16),
            pltpu.VMEM((TF, D_FF), _BF16),
        ],
        compiler_params=pltpu.CompilerParams(
            dimension_semantics=("arbitrary",), vmem_limit_bytes=FFN_VMEM_LIMIT),
        name="ffn_first" if first else ("ffn_final" if final else "ffn"),
    )(*args)


def _gmlp_kernel(x_ref, g_ref, wi_ref, wo_ref, vg_ref, vb_ref, ws_ref, bs_ref,
                 o_ref, vst_ref, win_s, wout_s, h_ref, uv_ref, vnb_ref, wsb_ref, y_ref,
                 *, n_w, n_prompt_tiles):
    i = pl.program_id(0)
    t = i - n_w
    wo_rows = D_A // n_w
    ncol = 2 * D_A // n_w

    @pl.when(i < n_w)
    def _load_weights():
        win_s[i] = wi_ref[...].astype(_BF16)
        wout_s[pl.ds(pl.multiple_of(i * wo_rows, wo_rows), wo_rows), :] = wo_ref[...].astype(_BF16)

    @pl.when(i >= n_w)
    def _tile():
        row = lax.broadcasted_iota(jnp.int32, (CHUNK_A, CHUNK_A), 0)
        col = lax.broadcasted_iota(jnp.int32, (CHUNK_A, CHUNK_A), 1)
        same_stream = jnp.logical_or(t < n_prompt_tiles, (row // RB) == (col // RB))
        keep = jnp.logical_and(col <= row, same_stream)
        for gi in range(N_GROUPS_A):
            wsb_ref[gi] = jnp.where(keep, ws_ref[0, gi], 0.0).astype(_BF16)

        for rows in _blocks(0, TM, RB):
            h_ref[rows, :] = _rms_rows(x_ref[rows, :], g_ref).astype(_BF16)

        n_half = n_w // 2
        for c in list(range(n_half, n_w)) + list(range(n_half)):
            tt = jnp.dot(h_ref[...], win_s[c], preferred_element_type=_F32)
            uv_ref[:, c * ncol:(c + 1) * ncol] = 0.5 * tt * (1.0 + lax.erf(tt * (2.0 ** -0.5)))
            if c == n_w - 1:
                for rows in _blocks(0, TM, LN_RB):
                    vn = _layer_norm_rows(uv_ref[rows, D_A:], vg_ref, vb_ref)
                    vst_ref[0, rows, :] = vn
                    vnb_ref[rows, :] = vn.astype(_BF16)

        for rows in _blocks(0, TM, CHUNK_A):
            for gi in range(N_GROUPS_A):
                cols = slice(gi * D_GROUP_A, (gi + 1) * D_GROUP_A)
                z = jnp.dot(wsb_ref[gi], vnb_ref[rows, cols], preferred_element_type=_F32)
                z = z + bs_ref[0, gi]
                y_ref[rows, cols] = (uv_ref[rows, cols] * z).astype(_BF16)

        out = jnp.dot(y_ref[...], wout_s[...], preferred_element_type=_F32)
        o_ref[...] = x_ref[...] + out


def _gmlp_call(x, g, w_in, w_out, lead, vg, vb, ws2, bs2, n_prompt_tiles, tiles_per_stream):
    n_w = 8
    n_tiles = n_prompt_tiles + 1
    n_states = (n_tiles + tiles_per_stream - 1) // tiles_per_stream
    tile = pl.BlockSpec((TM, D_MODEL), _tile_map(n_w, n_tiles))
    group_map = lambda i: (jnp.maximum(i - n_w, 0) // n_prompt_tiles, 0, 0, 0)
    state_map = lambda i: (jnp.maximum(i - n_w, 0) // tiles_per_stream, 0, 0)
    return pl.pallas_call(
        functools.partial(_gmlp_kernel, n_w=n_w, n_prompt_tiles=n_prompt_tiles),
        grid=(n_w + n_tiles,),
        in_specs=[
            tile, _const_spec((1, D_MODEL)),
            _w_col_spec(lead, D_MODEL, 2 * D_A // n_w, n_w, 0),
            _w_row_spec(lead, D_A // n_w, D_MODEL, n_w),
            _const_spec((1, D_A)), _const_spec((1, D_A)),
            pl.BlockSpec((1, N_GROUPS_A, CHUNK_A, CHUNK_A), group_map),
            pl.BlockSpec((1, N_GROUPS_A, CHUNK_A, D_GROUP_A), group_map),
        ],
        out_specs=[tile, pl.BlockSpec((1, TM, D_A), state_map)],
        out_shape=[jax.ShapeDtypeStruct((n_tiles * TM, D_MODEL), _F32),
                   jax.ShapeDtypeStruct((n_states, TM, D_A), _F32)],
        scratch_shapes=[
            pltpu.VMEM((n_w, D_MODEL, 2 * D_A // n_w), _BF16),
            pltpu.VMEM((D_A, D_MODEL), _BF16),
            pltpu.VMEM((TM, D_MODEL), _BF16),
            pltpu.VMEM((TM, 2 * D_A), _F32),
            pltpu.VMEM((TM, D_A), _BF16),
            pltpu.VMEM((N_GROUPS_A, CHUNK_A, CHUNK_A), _BF16),
            pltpu.VMEM((TM, D_A), _BF16),
        ],
        compiler_params=pltpu.CompilerParams(
            dimension_semantics=("arbitrary",), vmem_limit_bytes=VMEM_LIMIT),
        name="gmlp",
    )(x, g, w_in, w_out, vg, vb, ws2, bs2)


def _conv_kernel(x_ref, g_ref, wa_ref, wb_ref, wo_ref, cache_ref, dww_ref, dwb_ref, lng_ref, lnb_ref,
                 o_ref, glu_ref, win_s, wout_s, h_ref, xp_ref, c_ref, cn_ref,
                 *, n_w, n_prompt_tiles, tiles_per_stream, n_dec, dec_seq):
    i = pl.program_id(0)
    t = i - n_w

    @pl.when(i < n_w)
    def _load_weights():
        win_s[i, :, :CW] = wa_ref[...].astype(_BF16)
        win_s[i, :, CW:] = wb_ref[...].astype(_BF16)
        wout_s[pl.ds(pl.multiple_of(i * CW, CW), CW), :] = wo_ref[...].astype(_BF16)

    def conv_block(src, dst, rb):
        for cb in range(D_CONV // LANES):
            cols = slice(cb * LANES, (cb + 1) * LANES)
            acc = jnp.broadcast_to(dwb_ref[:, cols], (rb, LANES))
            for k in range(CONV_W):
                o = k + CONV_OFF
                r, q = o % SUBLANES, o // SUBLANES
                acc = acc + xp_ref[r, pl.ds(src + q * SUBLANES, rb), cols] * dww_ref[k:k + 1, cols]
            c_ref[pl.ds(dst, rb), cols] = acc

    def conv_segment(copy, src, dst, rb):
        n_win = rb + HIST
        for cb in range(D_CONV // LANES):
            cols = slice(cb * LANES, (cb + 1) * LANES)
            win = xp_ref[copy, pl.ds(src, n_win), cols]
            acc = jnp.broadcast_to(dwb_ref[:, cols], (rb, LANES))
            for r in range(SUBLANES):
                offs = [o for o in range(r, HIST + 1, SUBLANES) if 0 <= o - CONV_OFF < CONV_W]
                shifted = pltpu.roll(win, n_win - r, 0) if r else win
                for o in offs:
                    k = o - CONV_OFF
                    acc = acc + shifted[o - r:o - r + rb, :] * dww_ref[k:k + 1, cols]
            c_ref[pl.ds(dst, rb), cols] = acc

    def rms_rows(rows):
        h_ref[rows, :] = _rms_rows(x_ref[rows, :], g_ref).astype(_BF16)

    def in_proj(rs, xp_row0):
        for c in range(D_CONV // CW):
            ag = jnp.dot(h_ref[rs, :], win_s[c], preferred_element_type=_F32)
            glu = ag[:, :CW] * jax.nn.sigmoid(ag[:, CW:])
            glu_ref[0, rs, c * CW:(c + 1) * CW] = glu
            if xp_row0 is not None:
                for r in range(SUBLANES):
                    xp_ref[r, xp_row0 - r:xp_row0 - r + rs.stop - rs.start, c * CW:(c + 1) * CW] = glu

    def ln_rows(rows):
        cn = _layer_norm_rows(c_ref[rows, :], lng_ref, lnb_ref)
        cn_ref[rows, :] = (cn * jax.nn.sigmoid(cn)).astype(_BF16)

    def out_proj(rs):
        out = jnp.dot(cn_ref[rs, :], wout_s[...], preferred_element_type=_F32)
        o_ref[rs, :] = x_ref[rs, :] + out

    @pl.when(jnp.logical_and(i >= n_w, t < n_prompt_tiles))
    def _prompt_tile():
        @pl.when(t % tiles_per_stream == 0)
        def _new_stream():
            for r in range(SUBLANES):
                xp_ref[r, 0:HIST, :] = jnp.zeros((HIST, D_CONV), _F32)
                xp_ref[r, TM:TM + HIST, :] = jnp.zeros((HIST, D_CONV), _F32)

        for hh in range(TM // HM):
            rs = slice(hh * HM, (hh + 1) * HM)
            for rows in _blocks(hh * HM, HM, RB):
                rms_rows(rows)
            in_proj(rs, HIST + hh * HM)
            for rows in _blocks(hh * HM, HM, CONV_RB):
                conv_block(rows.start, rows.start, CONV_RB)
            for rows in _blocks(hh * HM, HM, RB):
                ln_rows(rows)
            out_proj(rs)
        for r in range(SUBLANES):
            xp_ref[r, 0:HIST, :] = xp_ref[r, TM:TM + HIST, :]

    @pl.when(t >= n_prompt_tiles)
    def _sample_tile():
        def loop(n, rb, body):
            def step(r, carry):
                body(pl.ds(pl.multiple_of(r * rb, rb), rb))
                return carry
            lax.fori_loop(0, n, step, 0)

        loop(TM // RB, RB, rms_rows)
        in_proj(slice(0, TM), None)
        seg = HIST + dec_seq
        per_copy = TM // seg
        for s in range(n_dec):
            row0 = (s % per_copy) * seg
            xp_ref[s // per_copy, row0:row0 + HIST, :] = cache_ref[s]
            xp_ref[s // per_copy, row0 + HIST:row0 + seg, :] = (
                glu_ref[0, s * dec_seq:(s + 1) * dec_seq, :])

        def seg_body(s, carry):
            conv_segment(s // per_copy, pl.multiple_of((s % per_copy) * seg, seg),
                         pl.multiple_of(s * dec_seq, dec_seq), dec_seq)
            return carry
        lax.fori_loop(0, n_dec, seg_body, 0)
        loop(TM // RB, RB, ln_rows)
        out_proj(slice(0, TM))


def _conv_call(x, g, w_in, w_out, lead, cache_pad, dww, dwb, lng, lnb, n_prompt_tiles,
               tiles_per_stream):
    n_w = D_CONV // CW
    n_tiles = n_prompt_tiles + 1
    n_states = (n_tiles + tiles_per_stream - 1) // tiles_per_stream
    n_dec, hist, _ = cache_pad.shape
    dec_seq = TM // n_dec
    assert hist == HIST and dec_seq % SUBLANES == 0 and dec_seq >= CONV_W - 1
    assert n_dec * (HIST + dec_seq) <= SUBLANES * TM
    tile = pl.BlockSpec((TM, D_MODEL), _tile_map(n_w, n_tiles))
    state_map = lambda i: (jnp.maximum(i - n_w, 0) // tiles_per_stream, 0, 0)
    return pl.pallas_call(
        functools.partial(_conv_kernel, n_w=n_w, n_prompt_tiles=n_prompt_tiles,
                          tiles_per_stream=tiles_per_stream, n_dec=n_dec, dec_seq=dec_seq),
        grid=(n_w + n_tiles,),
        in_specs=[
            tile, _const_spec((1, D_MODEL)),
            _w_col_spec(lead, D_MODEL, CW, n_w, 0),
            _w_col_spec(lead, D_MODEL, CW, n_w, n_w),
            _w_row_spec(lead, CW, D_MODEL, n_w),
            _const_spec(cache_pad.shape), _const_spec(dww.shape), _const_spec((1, D_CONV)),
            _const_spec((1, D_CONV)), _const_spec((1, D_CONV)),
        ],
        out_specs=[tile, pl.BlockSpec((1, TM, D_CONV), state_map)],
        out_shape=[jax.ShapeDtypeStruct((n_tiles * TM, D_MODEL), _F32),
                   jax.ShapeDtypeStruct((n_states, TM, D_CONV), _F32)],
        scratch_shapes=[
            pltpu.VMEM((n_w, D_MODEL, 2 * CW), _BF16),
            pltpu.VMEM((D_CONV, D_MODEL), _BF16),
            pltpu.VMEM((TM, D_MODEL), _BF16),
            pltpu.VMEM((SUBLANES, HIST + TM, D_CONV), _F32),
            pltpu.VMEM((TM, D_CONV), _F32),
            pltpu.VMEM((TM, D_CONV), _BF16),
        ],
        compiler_params=pltpu.CompilerParams(
            dimension_semantics=("arbitrary",), vmem_limit_bytes=VMEM_LIMIT),
        name="convmod",
    )(x, g, w_in, w_in, w_out, cache_pad, dww, dwb, lng, lnb)


def kernel(x_prompt, x_sample, cache_conv, norm_g, ffn_w_in, ffn_w_out, a_w_in, a_v_ln_g, a_v_ln_b,
           a_w_s, a_b_s, a_w_out, b_w_in, b_dw_w, b_dw_b, b_ln_g, b_ln_b, b_w_out, final_norm_g):
    batch, seq, _ = x_prompt.shape
    n_dec, dec_seq, _ = x_sample.shape
    depth = norm_g.shape[0]
    assert seq % TM == 0 and n_dec * dec_seq == TM and dec_seq == RB and CHUNK_A % dec_seq == 0
    tiles_per_stream = seq // TM
    n_prompt_tiles = batch * tiles_per_stream

    xs = [x_prompt.reshape(batch * seq, D_MODEL), x_sample.reshape(n_dec * dec_seq, D_MODEL)]
    a_states, conv_states = [], []
    for i in range(depth):
        j = i // 2
        xs = [_ffn_call(xs, norm_g[i, 0][None], ffn_w_in, ffn_w_out, (i, 0), n_prompt_tiles)]
        if i % 2 == 0:
            rep = CHUNK_A // dec_seq
            ws2 = jnp.stack([a_w_s[j], jnp.tile(a_w_s[j][:, :dec_seq, :dec_seq], (1, rep, rep))])
            bs_p = jnp.broadcast_to(a_b_s[j][:, :, None], (N_GROUPS_A, CHUNK_A, D_GROUP_A))
            bs_s = jnp.broadcast_to(jnp.tile(a_b_s[j][:, :dec_seq], (1, rep))[:, :, None],
                                    (N_GROUPS_A, CHUNK_A, D_GROUP_A))
            x, vst = _gmlp_call(xs[0], norm_g[i, 1][None], a_w_in, a_w_out, (j,), a_v_ln_g[j][None],
                                a_v_ln_b[j][None], ws2, jnp.stack([bs_p, bs_s]),
                                n_prompt_tiles, tiles_per_stream)
            a_states.append(vst)
        else:
            cache_pad = jnp.pad(cache_conv[j], ((0, 0), (CONV_OFF, 0), (0, 0)))
            dww = jnp.pad(b_dw_w[j], ((0, HIST - CONV_W), (0, 0)))
            x, gst = _conv_call(xs[0], norm_g[i, 1][None], b_w_in, b_w_out, (j,), cache_pad, dww,
                                b_dw_b[j][None], b_ln_g[j][None], b_ln_b[j][None],
                                n_prompt_tiles, tiles_per_stream)
            conv_states.append(gst)
        last = i == depth - 1
        res = _ffn_call([x], norm_g[i, 2][None], ffn_w_in, ffn_w_out, (i, 1), n_prompt_tiles,
                        final_norm_g[None] if last else None)
        xs = res if last else [res]

    y_prompt = xs[0].reshape(batch, seq, D_MODEL)
    y_sample = xs[1].reshape(n_dec, dec_seq, D_MODEL)
    vst = jnp.stack(a_states)
    gst = jnp.stack(conv_states)
    v_prompt = vst[:, :batch, TM - CHUNK_A:, :]
    v_sample = vst[:, batch].reshape(-1, n_dec, dec_seq, D_A)
    g_prompt = gst[:, :batch, TM - (CONV_W - 1):, :]
    g_sample = gst[:, batch].reshape(-1, n_dec, dec_seq, D_CONV)[:, :, dec_seq - (CONV_W - 1):, :]
    return (y_prompt, y_sample, v_prompt, v_sample, g_prompt, g_sample)
```

```python
import functools

import jax
import jax.numpy as jnp
from jax import lax
from jax.experimental import pallas as pl
from jax.experimental.pallas import tpu as pltpu

D_MODEL = 1024
D_FF = 2816
D_A = 2 * D_MODEL
CHUNK_A = 128
N_GROUPS_A = 8
D_GROUP_A = D_A // N_GROUPS_A
D_CONV = D_MODEL
CONV_W = 31
EPS = 1e-6

TM = 512
TF = 1024
RB = 32
LN_RB = 16
CONV_RB = 64
CW = 256
LANES = 128
SUBLANES = 8
HIST = 32
CONV_OFF = HIST - (CONV_W - 1)
VMEM_LIMIT = 56 * 1024 * 1024
FFN_VMEM_LIMIT = 58 * 1024 * 1024

_F32 = jnp.float32
_BF16 = jnp.bfloat16


def _const_spec(shape):
    nd = len(shape)
    return pl.BlockSpec(shape, lambda i: (0,) * nd, pipeline_mode=pl.Buffered(1))


def _blocks(start, n_rows, rb):
    return [slice(start + r * rb, start + (r + 1) * rb) for r in range(n_rows // rb)]


def _rms_rows(x, g_ref):
    ms = jnp.mean(x * x, axis=-1, keepdims=True)
    return x * lax.rsqrt(ms + EPS) * g_ref[...]


def _layer_norm_rows(x, g_ref, b_ref):
    mu = jnp.mean(x, axis=-1, keepdims=True)
    xc = x - mu
    var = jnp.mean(xc * xc, axis=-1, keepdims=True)
    return xc * lax.rsqrt(var + EPS) * g_ref[...] + b_ref[...]


def _tile_map(n_w, n_clip):
    return lambda i: (jnp.clip(i - n_w, 0, n_clip - 1), 0)


def _w_col_spec(lead, rows, cols, n_w, offset):
    none = (None,) * len(lead)
    return pl.BlockSpec(none + (rows, cols), lambda i: lead + (0, offset + jnp.minimum(i, n_w - 1)))


def _w_row_spec(lead, rows, cols, n_w):
    none = (None,) * len(lead)
    return pl.BlockSpec(none + (rows, cols), lambda i: lead + (jnp.minimum(i, n_w - 1), 0))


def _ffn_kernel(*refs, n_w, n_big, first, final):
    refs = list(refs)
    big_in = refs.pop(0)
    small_in = refs.pop(0) if first else big_in
    g_ref, wa_ref, wb_ref, wo_ref = (refs.pop(0) for _ in range(4))
    gf_ref = refs.pop(0) if final else None
    big_out = refs.pop(0)
    small_out = refs.pop(0) if final else big_out
    win_s, wout_s, h_ref, act_ref = refs

    i = pl.program_id(0)

    def rms_rows(src_ref, rows):
        h_ref[rows, :] = _rms_rows(src_ref[rows, :], g_ref).astype(_BF16)

    def final_norm(dst_ref, rs):
        for rows in _blocks(rs.start, rs.stop - rs.start, RB):
            dst_ref[rows, :] = _rms_rows(dst_ref[rows, :], gf_ref)

    def half(src_ref, dst_ref, rs):
        for rows in _blocks(rs.start, rs.stop - rs.start, RB):
            rms_rows(src_ref, rows)
        for c in range(n_w):
            ab = jnp.dot(h_ref[rs, :], win_s[c], preferred_element_type=_F32)
            a = ab[:, :CW]
            b = ab[:, CW:]
            act_ref[rs, c * CW:(c + 1) * CW] = (a * jax.nn.sigmoid(a) * b).astype(_BF16)
        y = jnp.dot(act_ref[rs, :], wout_s[...], preferred_element_type=_F32)
        dst_ref[rs, :] = src_ref[rs, :] + 0.5 * y
        if final:
            final_norm(dst_ref, rs)

    halves = [slice(hh * TM, (hh + 1) * TM) for hh in range(TF // TM)]

    @pl.when(i < n_w)
    def _weights_and_tile0():
        win_s[i, :, :CW] = wa_ref[...].astype(_BF16)
        win_s[i, :, CW:] = wb_ref[...].astype(_BF16)
        w_rows = pl.ds(pl.multiple_of(i * CW, CW), CW)
        wout_s[w_rows, :] = wo_ref[...].astype(_BF16)

        @pl.when(i == 0)
        def _start_tile0():
            for rows in _blocks(0, TF, RB):
                rms_rows(big_in, rows)
                big_out[rows, :] = big_in[rows, :]

        for rs in halves:
            ab = jnp.dot(h_ref[rs, :], win_s[i], preferred_element_type=_F32)
            a = ab[:, :CW]
            b = ab[:, CW:]
            act = (a * jax.nn.sigmoid(a) * b).astype(_BF16)
            big_out[rs, :] += 0.5 * jnp.dot(act, wout_s[w_rows, :], preferred_element_type=_F32)

        if final:
            @pl.when(i == n_w - 1)
            def _finish_tile0():
                for rs in halves:
                    final_norm(big_out, rs)

    @pl.when(jnp.logical_and(i >= n_w, i < n_w + n_big - 1))
    def _prompt_tile():
        for rs in halves:
            half(big_in, big_out, rs)

    @pl.when(i == n_w + n_big - 1)
    def _sample_tile():
        half(small_in, small_out, slice(0, TM))


def _ffn_call(xs, g, w_in, w_out, lead, n_prompt_tiles, gf=None):
    first = len(xs) == 2
    final = gf is not None
    n_w = D_FF // CW
    n_big = n_prompt_tiles * TM // TF
    assert n_prompt_tiles * TM % TF == 0
    n_tok = (n_prompt_tiles + 1) * TM

    def big_map(n_clip):
        return lambda i: (jnp.clip(i - (n_w - 1), 0, n_clip - 1), 0)

    big_all = pl.BlockSpec((TF, D_MODEL), big_map(n_big + 1))
    big_prompt = pl.BlockSpec((TF, D_MODEL), big_map(n_big))
    small = pl.BlockSpec((TM, D_MODEL), lambda i: (0, 0))

    in_specs = [big_prompt, small] if first else [big_all]
    in_specs += [
        _const_spec((1, D_MODEL)),
        _w_col_spec(lead, D_MODEL, CW, n_w, 0),
        _w_col_spec(lead, D_MODEL, CW, n_w, n_w),
        _w_row_spec(lead, CW, D_MODEL, n_w),
    ]
    args = list(xs) + [g, w_in, w_in, w_out]
    if final:
        in_specs.append(_const_spec((1, D_MODEL)))
        args.append(gf)
        out_specs = [big_prompt, small]
        out_shape = [jax.ShapeDtypeStruct((n_prompt_tiles * TM, D_MODEL), _F32),
                     jax.ShapeDtypeStruct((TM, D_MODEL), _F32)]
    else:
        out_specs = big_all
        out_shape = jax.ShapeDtypeStruct((n_tok, D_MODEL), _F32)
    return pl.pallas_call(
        functools.partial(_ffn_kernel, n_w=n_w, n_big=n_big, first=first, final=final),
        grid=(n_w + n_big,),
        in_specs=in_specs,
        out_specs=out_specs,
        out_shape=out_shape,
        scratch_shapes=[
            pltpu.VMEM((n_w, D_MODEL, 2 * CW), _BF16),
            pltpu.VMEM((D_FF, D_MODEL), _BF16),
            pltpu.VMEM((TF, D_MODEL), _BF16),
            pltpu.VMEM((TF, D_FF), _BF16),
        ],
        compiler_params=pltpu.CompilerParams(
            dimension_semantics=("arbitrary",), vmem_limit_bytes=FFN_VMEM_LIMIT),
        name="ffn_first" if first else ("ffn_final" if final else "ffn"),
    )(*args)


def _gmlp_kernel(x_ref, g_ref, wi_ref, wo_ref, vg_ref, vb_ref, ws_ref, bs_ref,
                 o_ref, vst_ref, win_s, wout_s, h_ref, uv_ref, vnb_ref, wsb_ref, y_ref,
                 *, n_w, n_prompt_tiles):
    i = pl.program_id(0)
    t = i - n_w
    wo_rows = D_A // n_w
    ncol = 2 * D_A // n_w

    @pl.when(i < n_w)
    def _load_weights():
        win_s[i] = wi_ref[...].astype(_BF16)
        wout_s[pl.ds(pl.multiple_of(i * wo_rows, wo_rows), wo_rows), :] = wo_ref[...].astype(_BF16)

    @pl.when(i >= n_w)
    def _tile():
        row = lax.broadcasted_iota(jnp.int32, (CHUNK_A, CHUNK_A), 0)
        col = lax.broadcasted_iota(jnp.int32, (CHUNK_A, CHUNK_A), 1)
        same_stream = jnp.logical_or(t < n_prompt_tiles, (row // RB) == (col // RB))
        keep = jnp.logical_and(col <= row, same_stream)
        for gi in range(N_GROUPS_A):
            wsb_ref[gi] = jnp.where(keep, ws_ref[0, gi], 0.0).astype(_BF16)

        for rows in _blocks(0, TM, RB):
            h_ref[rows, :] = _rms_rows(x_ref[rows, :], g_ref).astype(_BF16)

        n_half = n_w // 2
        for c in list(range(n_half, n_w)) + list(range(n_half)):
            tt = jnp.dot(h_ref[...], win_s[c], preferred_element_type=_F32)
            uv_ref[:, c * ncol:(c + 1) * ncol] = 0.5 * tt * (1.0 + lax.erf(tt * (2.0 ** -0.5)))
            if c == n_w - 1:
                for rows in _blocks(0, TM, LN_RB):
                    vn = _layer_norm_rows(uv_ref[rows, D_A:], vg_ref, vb_ref)
                    vst_ref[0, rows, :] = vn
                    vnb_ref[rows, :] = vn.astype(_BF16)

        for rows in _blocks(0, TM, CHUNK_A):
            for gi in range(N_GROUPS_A):
                cols = slice(gi * D_GROUP_A, (gi + 1) * D_GROUP_A)
                z = jnp.dot(wsb_ref[gi], vnb_ref[rows, cols], preferred_element_type=_F32)
                z = z + bs_ref[0, gi]
                y_ref[rows, cols] = (uv_ref[rows, cols] * z).astype(_BF16)

        out = jnp.dot(y_ref[...], wout_s[...], preferred_element_type=_F32)
        o_ref[...] = x_ref[...] + out


def _gmlp_call(x, g, w_in, w_out, lead, vg, vb, ws2, bs2, n_prompt_tiles, tiles_per_stream):
    n_w = 8
    n_tiles = n_prompt_tiles + 1
    n_states = (n_tiles + tiles_per_stream - 1) // tiles_per_stream
    tile = pl.BlockSpec((TM, D_MODEL), _tile_map(n_w, n_tiles))
    group_map = lambda i: (jnp.maximum(i - n_w, 0) // n_prompt_tiles, 0, 0, 0)
    state_map = lambda i: (jnp.maximum(i - n_w, 0) // tiles_per_stream, 0, 0)
    return pl.pallas_call(
        functools.partial(_gmlp_kernel, n_w=n_w, n_prompt_tiles=n_prompt_tiles),
        grid=(n_w + n_tiles,),
        in_specs=[
            tile, _const_spec((1, D_MODEL)),
            _w_col_spec(lead, D_MODEL, 2 * D_A // n_w, n_w, 0),
            _w_row_spec(lead, D_A // n_w, D_MODEL, n_w),
            _const_spec((1, D_A)), _const_spec((1, D_A)),
            pl.BlockSpec((1, N_GROUPS_A, CHUNK_A, CHUNK_A), group_map),
            pl.BlockSpec((1, N_GROUPS_A, CHUNK_A, D_GROUP_A), group_map),
        ],
        out_specs=[tile, pl.BlockSpec((1, TM, D_A), state_map)],
        out_shape=[jax.ShapeDtypeStruct((n_tiles * TM, D_MODEL), _F32),
                   jax.ShapeDtypeStruct((n_states, TM, D_A), _F32)],
        scratch_shapes=[
            pltpu.VMEM((n_w, D_MODEL, 2 * D_A // n_w), _BF16),
            pltpu.VMEM((D_A, D_MODEL), _BF16),
            pltpu.VMEM((TM, D_MODEL), _BF16),
            pltpu.VMEM((TM, 2 * D_A), _F32),
            pltpu.VMEM((TM, D_A), _BF16),
            pltpu.VMEM((N_GROUPS_A, CHUNK_A, CHUNK_A), _BF16),
            pltpu.VMEM((TM, D_A), _BF16),
        ],
        compiler_params=pltpu.CompilerParams(
            dimension_semantics=("arbitrary",), vmem_limit_bytes=VMEM_LIMIT),
        name="gmlp",
    )(x, g, w_in, w_out, vg, vb, ws2, bs2)


def _conv_kernel(x_ref, g_ref, wa_ref, wb_ref, wo_ref, cache_ref, dww_ref, dwb_ref, lng_ref, lnb_ref,
                 o_ref, glu_ref, win_s, wout_s, h_ref, xp_ref, c_ref, cn_ref,
                 *, n_w, n_prompt_tiles, tiles_per_stream, n_dec, dec_seq):
    i = pl.program_id(0)
    t = i - n_w

    @pl.when(i < n_w)
    def _load_weights():
        win_s[i, :, :CW] = wa_ref[...].astype(_BF16)
        win_s[i, :, CW:] = wb_ref[...].astype(_BF16)
        wout_s[pl.ds(pl.multiple_of(i * CW, CW), CW), :] = wo_ref[...].astype(_BF16)

    def conv_block(src, dst, rb):
        for cb in range(D_CONV // LANES):
            cols = slice(cb * LANES, (cb + 1) * LANES)
            acc = jnp.broadcast_to(dwb_ref[:, cols], (rb, LANES))
            for k in range(CONV_W):
                o = k + CONV_OFF
                r, q = o % SUBLANES, o // SUBLANES
                acc = acc + xp_ref[r, pl.ds(src + q * SUBLANES, rb), cols] * dww_ref[k:k + 1, cols]
            c_ref[pl.ds(dst, rb), cols] = acc

    def conv_segment(copy, src, dst, rb):
        n_win = rb + HIST
        for cb in range(D_CONV // LANES):
            cols = slice(cb * LANES, (cb + 1) * LANES)
            win = xp_ref[copy, pl.ds(src, n_win), cols]
            acc = jnp.broadcast_to(dwb_ref[:, cols], (rb, LANES))
            for r in range(SUBLANES):
                offs = [o for o in range(r, HIST + 1, SUBLANES) if 0 <= o - CONV_OFF < CONV_W]
                shifted = pltpu.roll(win, n_win - r, 0) if r else win
                for o in offs:
                    k = o - CONV_OFF
                    acc = acc + shifted[o - r:o - r + rb, :] * dww_ref[k:k + 1, cols]
            c_ref[pl.ds(dst, rb), cols] = acc

    def rms_rows(rows):
        h_ref[rows, :] = _rms_rows(x_ref[rows, :], g_ref).astype(_BF16)

    def in_proj(rs, xp_row0):
        for c in range(D_CONV // CW):
            ag = jnp.dot(h_ref[rs, :], win_s[c], preferred_element_type=_F32)
            glu = ag[:, :CW] * jax.nn.sigmoid(ag[:, CW:])
            glu_ref[0, rs, c * CW:(c + 1) * CW] = glu
            if xp_row0 is not None:
                for r in range(SUBLANES):
                    xp_ref[r, xp_row0 - r:xp_row0 - r + rs.stop - rs.start, c * CW:(c + 1) * CW] = glu

    def ln_rows(rows):
        cn = _layer_norm_rows(c_ref[rows, :], lng_ref, lnb_ref)
        cn_ref[rows, :] = (cn * jax.nn.sigmoid(cn)).astype(_BF16)

    def out_proj(rs):
        out = jnp.dot(cn_ref[rs, :], wout_s[...], preferred_element_type=_F32)
        o_ref[rs, :] = x_ref[rs, :] + out

    @pl.when(jnp.logical_and(i >= n_w, t < n_prompt_tiles))
    def _prompt_tile():
        @pl.when(t % tiles_per_stream == 0)
        def _new_stream():
            for r in range(SUBLANES):
                xp_ref[r, 0:HIST, :] = jnp.zeros((HIST, D_CONV), _F32)
                xp_ref[r, TM:TM + HIST, :] = jnp.zeros((HIST, D_CONV), _F32)

        tile = slice(0, TM)
        for rows in _blocks(0, TM, RB):
            rms_rows(rows)
        in_proj(tile, HIST)
        for rows in _blocks(0, TM, CONV_RB):
            conv_block(rows.start, rows.start, CONV_RB)
        for rows in _blocks(0, TM, RB):
            ln_rows(rows)
        out_proj(tile)
        for r in range(SUBLANES):
            xp_ref[r, 0:HIST, :] = xp_ref[r, TM:TM + HIST, :]

    @pl.when(t >= n_prompt_tiles)
    def _sample_tile():
        def loop(n, rb, body):
            def step(r, carry):
                body(pl.ds(pl.multiple_of(r * rb, rb), rb))
                return carry
            lax.fori_loop(0, n, step, 0)

        loop(TM // RB, RB, rms_rows)
        in_proj(slice(0, TM), None)
        seg = HIST + dec_seq
        per_copy = TM // seg
        for s in range(n_dec):
            row0 = (s % per_copy) * seg
            xp_ref[s // per_copy, row0:row0 + HIST, :] = cache_ref[s]
            xp_ref[s // per_copy, row0 + HIST:row0 + seg, :] = (
                glu_ref[0, s * dec_seq:(s + 1) * dec_seq, :])

        def seg_body(s, carry):
            conv_segment(s // per_copy, pl.multiple_of((s % per_copy) * seg, seg),
                         pl.multiple_of(s * dec_seq, dec_seq), dec_seq)
            return carry
        lax.fori_loop(0, n_dec, seg_body, 0)
        loop(TM // RB, RB, ln_rows)
        out_proj(slice(0, TM))


def _conv_call(x, g, w_in, w_out, lead, cache_pad, dww, dwb, lng, lnb, n_prompt_tiles,
               tiles_per_stream):
    n_w = D_CONV // CW
    n_tiles = n_prompt_tiles + 1
    n_states = (n_tiles + tiles_per_stream - 1) // tiles_per_stream
    n_dec, hist, _ = cache_pad.shape
    dec_seq = TM // n_dec
    assert hist == HIST and dec_seq % SUBLANES == 0 and dec_seq >= CONV_W - 1
    assert n_dec * (HIST + dec_seq) <= SUBLANES * TM
    tile = pl.BlockSpec((TM, D_MODEL), _tile_map(n_w, n_tiles))
    state_map = lambda i: (jnp.maximum(i - n_w, 0) // tiles_per_stream, 0, 0)
    return pl.pallas_call(
        functools.partial(_conv_kernel, n_w=n_w, n_prompt_tiles=n_prompt_tiles,
                          tiles_per_stream=tiles_per_stream, n_dec=n_dec, dec_seq=dec_seq),
        grid=(n_w + n_tiles,),
        in_specs=[
            tile, _const_spec((1, D_MODEL)),
            _w_col_spec(lead, D_MODEL, CW, n_w, 0),
            _w_col_spec(lead, D_MODEL, CW, n_w, n_w),
            _w_row_spec(lead, CW, D_MODEL, n_w),
            _const_spec(cache_pad.shape), _const_spec(dww.shape), _const_spec((1, D_CONV)),
            _const_spec((1, D_CONV)), _const_spec((1, D_CONV)),
        ],
        out_specs=[tile, pl.BlockSpec((1, TM, D_CONV), state_map)],
        out_shape=[jax.ShapeDtypeStruct((n_tiles * TM, D_MODEL), _F32),
                   jax.ShapeDtypeStruct((n_states, TM, D_CONV), _F32)],
        scratch_shapes=[
            pltpu.VMEM((n_w, D_MODEL, 2 * CW), _BF16),
            pltpu.VMEM((D_CONV, D_MODEL), _BF16),
            pltpu.VMEM((TM, D_MODEL), _BF16),
            pltpu.VMEM((SUBLANES, HIST + TM, D_CONV), _F32),
            pltpu.VMEM((TM, D_CONV), _F32),
            pltpu.VMEM((TM, D_CONV), _BF16),
        ],
        compiler_params=pltpu.CompilerParams(
            dimension_semantics=("arbitrary",), vmem_limit_bytes=VMEM_LIMIT),
        name="convmod",
    )(x, g, w_in, w_in, w_out, cache_pad, dww, dwb, lng, lnb)


def kernel(x_prompt, x_sample, cache_conv, norm_g, ffn_w_in, ffn_w_out, a_w_in, a_v_ln_g, a_v_ln_b,
           a_w_s, a_b_s, a_w_out, b_w_in, b_dw_w, b_dw_b, b_ln_g, b_ln_b, b_w_out, final_norm_g):
    batch, seq, _ = x_prompt.shape
    n_dec, dec_seq, _ = x_sample.shape
    depth = norm_g.shape[0]
    assert seq % TM == 0 and n_dec * dec_seq == TM and dec_seq == RB and CHUNK_A % dec_seq == 0
    tiles_per_stream = seq // TM
    n_prompt_tiles = batch * tiles_per_stream

    xs = [x_prompt.reshape(batch * seq, D_MODEL), x_sample.reshape(n_dec * dec_seq, D_MODEL)]
    a_states, conv_states = [], []
    for i in range(depth):
        j = i // 2
        xs = [_ffn_call(xs, norm_g[i, 0][None], ffn_w_in, ffn_w_out, (i, 0), n_prompt_tiles)]
        if i % 2 == 0:
            rep = CHUNK_A // dec_seq
            ws2 = jnp.stack([a_w_s[j], jnp.tile(a_w_s[j][:, :dec_seq, :dec_seq], (1, rep, rep))])
            bs_p = jnp.broadcast_to(a_b_s[j][:, :, None], (N_GROUPS_A, CHUNK_A, D_GROUP_A))
            bs_s = jnp.broadcast_to(jnp.tile(a_b_s[j][:, :dec_seq], (1, rep))[:, :, None],
                                    (N_GROUPS_A, CHUNK_A, D_GROUP_A))
            x, vst = _gmlp_call(xs[0], norm_g[i, 1][None], a_w_in, a_w_out, (j,), a_v_ln_g[j][None],
                                a_v_ln_b[j][None], ws2, jnp.stack([bs_p, bs_s]),
                                n_prompt_tiles, tiles_per_stream)
            a_states.append(vst)
        else:
            cache_pad = jnp.pad(cache_conv[j], ((0, 0), (CONV_OFF, 0), (0, 0)))
            dww = jnp.pad(b_dw_w[j], ((0, HIST - CONV_W), (0, 0)))
            x, gst = _conv_call(xs[0], norm_g[i, 1][None], b_w_in, b_w_out, (j,), cache_pad, dww,
                                b_dw_b[j][None], b_ln_g[j][None], b_ln_b[j][None],
                                n_prompt_tiles, tiles_per_stream)
            conv_states.append(gst)
        last = i == depth - 1
        res = _ffn_call([x], norm_g[i, 2][None], ffn_w_in, ffn_w_out, (i, 1), n_prompt_tiles,
                        final_norm_g[None] if last else None)
        xs = res if last else [res]

    y_prompt = xs[0].reshape(batch, seq, D_MODEL)
    y_sample = xs[1].reshape(n_dec, dec_seq, D_MODEL)
    vst = jnp.stack(a_states)
    gst = jnp.stack(conv_states)
    v_prompt = vst[:, :batch, TM - CHUNK_A:, :]
    v_sample = vst[:, batch].reshape(-1, n_dec, dec_seq, D_A)
    g_prompt = gst[:, :batch, TM - (CONV_W - 1):, :]
    g_sample = gst[:, batch].reshape(-1, n_dec, dec_seq, D_CONV)[:, :, dec_seq - (CONV_W - 1):, :]
    return (y_prompt, y_sample, v_prompt, v_sample, g_prompt, g_sample)
```

```python
import functools

import jax
import jax.numpy as jnp
from jax import lax
from jax.experimental import pallas as pl
from jax.experimental.pallas import tpu as pltpu

D_MODEL = 1024
D_FF = 2816
D_A = 2 * D_MODEL
CHUNK_A = 128
N_GROUPS_A = 8
D_GROUP_A = D_A // N_GROUPS_A
D_CONV = D_MODEL
CONV_W = 31
EPS = 1e-6

TM = 512
TF = 1024
RB = 32
LN_RB = 16
CONV_RB = 64
CW = 256
LANES = 128
SUBLANES = 8
HIST = 32
CONV_OFF = HIST - (CONV_W - 1)
VMEM_LIMIT = 56 * 1024 * 1024
FFN_VMEM_LIMIT = 58 * 1024 * 1024

_F32 = jnp.float32
_BF16 = jnp.bfloat16


def _const_spec(shape):
    nd = len(shape)
    return pl.BlockSpec(shape, lambda i: (0,) * nd, pipeline_mode=pl.Buffered(1))


def _blocks(start, n_rows, rb):
    return [slice(start + r * rb, start + (r + 1) * rb) for r in range(n_rows // rb)]


def _rms_rows(x, g_ref):
    ms = jnp.mean(x * x, axis=-1, keepdims=True)
    return x * lax.rsqrt(ms + EPS) * g_ref[...]


def _layer_norm_rows(x, g_ref, b_ref):
    mu = jnp.mean(x, axis=-1, keepdims=True)
    xc = x - mu
    var = jnp.mean(xc * xc, axis=-1, keepdims=True)
    return xc * lax.rsqrt(var + EPS) * g_ref[...] + b_ref[...]


def _tile_map(n_w, n_clip):
    return lambda i: (jnp.clip(i - n_w, 0, n_clip - 1), 0)


def _w_col_spec(lead, rows, cols, n_w, offset):
    none = (None,) * len(lead)
    return pl.BlockSpec(none + (rows, cols), lambda i: lead + (0, offset + jnp.minimum(i, n_w - 1)))


def _w_row_spec(lead, rows, cols, n_w):
    none = (None,) * len(lead)
    return pl.BlockSpec(none + (rows, cols), lambda i: lead + (jnp.minimum(i, n_w - 1), 0))


def _ffn_kernel(*refs, n_w, n_big, first, final):
    refs = list(refs)
    big_in = refs.pop(0)
    small_in = refs.pop(0) if first else big_in
    g_ref, wa_ref, wb_ref, wo_ref = (refs.pop(0) for _ in range(4))
    gf_ref = refs.pop(0) if final else None
    big_out = refs.pop(0)
    small_out = refs.pop(0) if final else big_out
    win_s, wout_s, h_ref, act_ref = refs

    i = pl.program_id(0)

    def rms_rows(src_ref, rows):
        h_ref[rows, :] = _rms_rows(src_ref[rows, :], g_ref).astype(_BF16)

    def final_norm(dst_ref, rs):
        for rows in _blocks(rs.start, rs.stop - rs.start, RB):
            dst_ref[rows, :] = _rms_rows(dst_ref[rows, :], gf_ref)

    def half(src_ref, dst_ref, rs):
        for rows in _blocks(rs.start, rs.stop - rs.start, RB):
            rms_rows(src_ref, rows)
        for c in range(n_w):
            ab = jnp.dot(h_ref[rs, :], win_s[c], preferred_element_type=_F32)
            a = ab[:, :CW]
            b = ab[:, CW:]
            act_ref[rs, c * CW:(c + 1) * CW] = (a * jax.nn.sigmoid(a) * b).astype(_BF16)
        y = jnp.dot(act_ref[rs, :], wout_s[...], preferred_element_type=_F32)
        dst_ref[rs, :] = src_ref[rs, :] + 0.5 * y
        if final:
            final_norm(dst_ref, rs)

    halves = [slice(hh * TM, (hh + 1) * TM) for hh in range(TF // TM)]

    @pl.when(i < n_w)
    def _weights_and_tile0():
        win_s[i, :, :CW] = wa_ref[...].astype(_BF16)
        win_s[i, :, CW:] = wb_ref[...].astype(_BF16)
        w_rows = pl.ds(pl.multiple_of(i * CW, CW), CW)
        wout_s[w_rows, :] = wo_ref[...].astype(_BF16)

        @pl.when(i == 0)
        def _start_tile0():
            for rows in _blocks(0, TF, RB):
                rms_rows(big_in, rows)
                big_out[rows, :] = big_in[rows, :]

        for rs in halves:
            ab = jnp.dot(h_ref[rs, :], win_s[i], preferred_element_type=_F32)
            a = ab[:, :CW]
            b = ab[:, CW:]
            act = (a * jax.nn.sigmoid(a) * b).astype(_BF16)
            big_out[rs, :] += 0.5 * jnp.dot(act, wout_s[w_rows, :], preferred_element_type=_F32)

        if final:
            @pl.when(i == n_w - 1)
            def _finish_tile0():
                for rs in halves:
                    final_norm(big_out, rs)

    @pl.when(jnp.logical_and(i >= n_w, i < n_w + n_big - 1))
    def _prompt_tile():
        for rs in halves:
            half(big_in, big_out, rs)

    @pl.when(i == n_w + n_big - 1)
    def _sample_tile():
        half(small_in, small_out, slice(0, TM))


def _ffn_call(xs, g, w_in, w_out, lead, n_prompt_tiles, gf=None):
    first = len(xs) == 2
    final = gf is not None
    n_w = D_FF // CW
    n_big = n_prompt_tiles * TM // TF
    assert n_prompt_tiles * TM % TF == 0
    n_tok = (n_prompt_tiles + 1) * TM

    def big_map(n_clip):
        return lambda i: (jnp.clip(i - (n_w - 1), 0, n_clip - 1), 0)

    big_all = pl.BlockSpec((TF, D_MODEL), big_map(n_big + 1))
    big_prompt = pl.BlockSpec((TF, D_MODEL), big_map(n_big))
    small = pl.BlockSpec((TM, D_MODEL), lambda i: (0, 0))

    in_specs = [big_prompt, small] if first else [big_all]
    in_specs += [
        _const_spec((1, D_MODEL)),
        _w_col_spec(lead, D_MODEL, CW, n_w, 0),
        _w_col_spec(lead, D_MODEL, CW, n_w, n_w),
        _w_row_spec(lead, CW, D_MODEL, n_w),
    ]
    args = list(xs) + [g, w_in, w_in, w_out]
    if final:
        in_specs.append(_const_spec((1, D_MODEL)))
        args.append(gf)
        out_specs = [big_prompt, small]
        out_shape = [jax.ShapeDtypeStruct((n_prompt_tiles * TM, D_MODEL), _F32),
                     jax.ShapeDtypeStruct((TM, D_MODEL), _F32)]
    else:
        out_specs = big_all
        out_shape = jax.ShapeDtypeStruct((n_tok, D_MODEL), _F32)
    return pl.pallas_call(
        functools.partial(_ffn_kernel, n_w=n_w, n_big=n_big, first=first, final=final),
        grid=(n_w + n_big,),
        in_specs=in_specs,
        out_specs=out_specs,
        out_shape=out_shape,
        scratch_shapes=[
            pltpu.VMEM((n_w, D_MODEL, 2 * CW), _BF16),
            pltpu.VMEM((D_FF, D_MODEL), _BF16),
            pltpu.VMEM((TF, D_MODEL), _BF16),
            pltpu.VMEM((TF, D_FF), _BF16),
        ],
        compiler_params=pltpu.CompilerParams(
            dimension_semantics=("arbitrary",), vmem_limit_bytes=FFN_VMEM_LIMIT),
        name="ffn_first" if first else ("ffn_final" if final else "ffn"),
    )(*args)


def _gmlp_kernel(x_ref, g_ref, wi_ref, wo_ref, vg_ref, vb_ref, ws_ref, bs_ref,
                 o_ref, vp_ref, vs_ref, win_s, wout_s, h_ref, uv_ref, vnb_ref, wsb_ref, y_ref,
                 *, n_w, n_prompt_tiles):
    i = pl.program_id(0)
    t = i - n_w
    wo_rows = D_A // n_w
    ncol = 2 * D_A // n_w

    @pl.when(i < n_w)
    def _load_weights():
        win_s[i] = wi_ref[...].astype(_BF16)
        wout_s[pl.ds(pl.multiple_of(i * wo_rows, wo_rows), wo_rows), :] = wo_ref[...].astype(_BF16)

    @pl.when(i >= n_w)
    def _tile():
        row = lax.broadcasted_iota(jnp.int32, (CHUNK_A, CHUNK_A), 0)
        col = lax.broadcasted_iota(jnp.int32, (CHUNK_A, CHUNK_A), 1)
        same_stream = jnp.logical_or(t < n_prompt_tiles, (row // RB) == (col // RB))
        keep = jnp.logical_and(col <= row, same_stream)
        for gi in range(N_GROUPS_A):
            wsb_ref[gi] = jnp.where(keep, ws_ref[0, gi], 0.0).astype(_BF16)

        for rows in _blocks(0, TM, RB):
            h_ref[rows, :] = _rms_rows(x_ref[rows, :], g_ref).astype(_BF16)

        n_half = n_w // 2
        for c in list(range(n_half, n_w)) + list(range(n_half)):
            tt = jnp.dot(h_ref[...], win_s[c], preferred_element_type=_F32)
            uv_ref[:, c * ncol:(c + 1) * ncol] = 0.5 * tt * (1.0 + lax.erf(tt * (2.0 ** -0.5)))
            if c == n_w - 1:
                for rows in _blocks(0, TM, LN_RB):
                    vn = _layer_norm_rows(uv_ref[rows, D_A:], vg_ref, vb_ref)
                    vs_ref[rows, :] = vn
                    vnb_ref[rows, :] = vn.astype(_BF16)

        for rows in _blocks(0, TM, CHUNK_A):
            for gi in range(N_GROUPS_A):
                cols = slice(gi * D_GROUP_A, (gi + 1) * D_GROUP_A)
                z = jnp.dot(wsb_ref[gi], vnb_ref[rows, cols], preferred_element_type=_F32)
                z = z + bs_ref[0, gi]
                y_ref[rows, cols] = (uv_ref[rows, cols] * z).astype(_BF16)

        out = jnp.dot(y_ref[...], wout_s[...], preferred_element_type=_F32)
        o_ref[...] = x_ref[...] + out

        @pl.when(t < n_prompt_tiles)
        def _prompt_state():
            vp_ref[0] = vs_ref[TM - CHUNK_A:TM, :]


def _gmlp_call(x, g, w_in, w_out, lead, vg, vb, ws2, bs2, n_prompt_tiles, tiles_per_stream):
    n_w = 8
    n_tiles = n_prompt_tiles + 1
    n_streams = n_prompt_tiles // tiles_per_stream
    tile = pl.BlockSpec((TM, D_MODEL), _tile_map(n_w, n_tiles))
    group_map = lambda i: (jnp.maximum(i - n_w, 0) // n_prompt_tiles, 0, 0, 0)
    stream_map = lambda i: (jnp.clip((i - n_w) // tiles_per_stream, 0, n_streams - 1), 0, 0)
    return pl.pallas_call(
        functools.partial(_gmlp_kernel, n_w=n_w, n_prompt_tiles=n_prompt_tiles),
        grid=(n_w + n_tiles,),
        in_specs=[
            tile, _const_spec((1, D_MODEL)),
            _w_col_spec(lead, D_MODEL, 2 * D_A // n_w, n_w, 0),
            _w_row_spec(lead, D_A // n_w, D_MODEL, n_w),
            _const_spec((1, D_A)), _const_spec((1, D_A)),
            pl.BlockSpec((1, N_GROUPS_A, CHUNK_A, CHUNK_A), group_map),
            pl.BlockSpec((1, N_GROUPS_A, CHUNK_A, D_GROUP_A), group_map),
        ],
        out_specs=[tile, pl.BlockSpec((1, CHUNK_A, D_A), stream_map),
                   pl.BlockSpec((TM, D_A), lambda i: (0, 0))],
        out_shape=[jax.ShapeDtypeStruct((n_tiles * TM, D_MODEL), _F32),
                   jax.ShapeDtypeStruct((n_streams, CHUNK_A, D_A), _F32),
                   jax.ShapeDtypeStruct((TM, D_A), _F32)],
        scratch_shapes=[
            pltpu.VMEM((n_w, D_MODEL, 2 * D_A // n_w), _BF16),
            pltpu.VMEM((D_A, D_MODEL), _BF16),
            pltpu.VMEM((TM, D_MODEL), _BF16),
            pltpu.VMEM((TM, 2 * D_A), _F32),
            pltpu.VMEM((TM, D_A), _BF16),
            pltpu.VMEM((N_GROUPS_A, CHUNK_A, CHUNK_A), _BF16),
            pltpu.VMEM((TM, D_A), _BF16),
        ],
        compiler_params=pltpu.CompilerParams(
            dimension_semantics=("arbitrary",), vmem_limit_bytes=VMEM_LIMIT),
        name="gmlp",
    )(x, g, w_in, w_out, vg, vb, ws2, bs2)


def _conv_kernel(x_ref, g_ref, wa_ref, wb_ref, wo_ref, cache_ref, dww_ref, dwb_ref, lng_ref, lnb_ref,
                 o_ref, glu_ref, win_s, wout_s, h_ref, xp_ref, c_ref, cn_ref,
                 *, n_w, n_prompt_tiles, tiles_per_stream, n_dec, dec_seq):
    i = pl.program_id(0)
    t = i - n_w

    @pl.when(i < n_w)
    def _load_weights():
        win_s[i, :, :CW] = wa_ref[...].astype(_BF16)
        win_s[i, :, CW:] = wb_ref[...].astype(_BF16)
        wout_s[pl.ds(pl.multiple_of(i * CW, CW), CW), :] = wo_ref[...].astype(_BF16)

    def conv_block(src, dst, rb):
        for cb in range(D_CONV // LANES):
            cols = slice(cb * LANES, (cb + 1) * LANES)
            acc = jnp.broadcast_to(dwb_ref[:, cols], (rb, LANES))
            for k in range(CONV_W):
                o = k + CONV_OFF
                r, q = o % SUBLANES, o // SUBLANES
                acc = acc + xp_ref[r, pl.ds(src + q * SUBLANES, rb), cols] * dww_ref[k:k + 1, cols]
            c_ref[pl.ds(dst, rb), cols] = acc

    def conv_segment(copy, src, dst, rb):
        n_win = rb + HIST
        for cb in range(D_CONV // LANES):
            cols = slice(cb * LANES, (cb + 1) * LANES)
            win = xp_ref[copy, pl.ds(src, n_win), cols]
            acc = jnp.broadcast_to(dwb_ref[:, cols], (rb, LANES))
            for r in range(SUBLANES):
                offs = [o for o in range(r, HIST + 1, SUBLANES) if 0 <= o - CONV_OFF < CONV_W]
                shifted = pltpu.roll(win, n_win - r, 0) if r else win
                for o in offs:
                    k = o - CONV_OFF
                    acc = acc + shifted[o - r:o - r + rb, :] * dww_ref[k:k + 1, cols]
            c_ref[pl.ds(dst, rb), cols] = acc

    def rms_rows(rows):
        h_ref[rows, :] = _rms_rows(x_ref[rows, :], g_ref).astype(_BF16)

    def in_proj(rs, xp_row0):
        for c in range(D_CONV // CW):
            ag = jnp.dot(h_ref[rs, :], win_s[c], preferred_element_type=_F32)
            glu = ag[:, :CW] * jax.nn.sigmoid(ag[:, CW:])
            glu_ref[0, rs, c * CW:(c + 1) * CW] = glu
            if xp_row0 is not None:
                for r in range(SUBLANES):
                    xp_ref[r, xp_row0 - r:xp_row0 - r + rs.stop - rs.start, c * CW:(c + 1) * CW] = glu

    def ln_rows(rows):
        cn = _layer_norm_rows(c_ref[rows, :], lng_ref, lnb_ref)
        cn_ref[rows, :] = (cn * jax.nn.sigmoid(cn)).astype(_BF16)

    def out_proj(rs):
        out = jnp.dot(cn_ref[rs, :], wout_s[...], preferred_element_type=_F32)
        o_ref[rs, :] = x_ref[rs, :] + out

    @pl.when(jnp.logical_and(i >= n_w, t < n_prompt_tiles))
    def _prompt_tile():
        @pl.when(t % tiles_per_stream == 0)
        def _new_stream():
            for r in range(SUBLANES):
                xp_ref[r, 0:HIST, :] = jnp.zeros((HIST, D_CONV), _F32)
                xp_ref[r, TM:TM + HIST, :] = jnp.zeros((HIST, D_CONV), _F32)

        tile = slice(0, TM)
        for rows in _blocks(0, TM, RB):
            rms_rows(rows)
        in_proj(tile, HIST)
        for rows in _blocks(0, TM, CONV_RB):
            conv_block(rows.start, rows.start, CONV_RB)
        for rows in _blocks(0, TM, RB):
            ln_rows(rows)
        out_proj(tile)
        for r in range(SUBLANES):
            xp_ref[r, 0:HIST, :] = xp_ref[r, TM:TM + HIST, :]

    @pl.when(t >= n_prompt_tiles)
    def _sample_tile():
        def loop(n, rb, body):
            def step(r, carry):
                body(pl.ds(pl.multiple_of(r * rb, rb), rb))
                return carry
            lax.fori_loop(0, n, step, 0)

        loop(TM // RB, RB, rms_rows)
        in_proj(slice(0, TM), None)
        seg = HIST + dec_seq
        per_copy = TM // seg
        for s in range(n_dec):
            row0 = (s % per_copy) * seg
            xp_ref[s // per_copy, row0:row0 + HIST, :] = cache_ref[s]
            xp_ref[s // per_copy, row0 + HIST:row0 + seg, :] = (
                glu_ref[0, s * dec_seq:(s + 1) * dec_seq, :])

        def seg_body(s, carry):
            conv_segment(s // per_copy, pl.multiple_of((s % per_copy) * seg, seg),
                         pl.multiple_of(s * dec_seq, dec_seq), dec_seq)
            return carry
        lax.fori_loop(0, n_dec, seg_body, 0)
        loop(TM // RB, RB, ln_rows)
        out_proj(slice(0, TM))


def _conv_call(x, g, w_in, w_out, lead, cache_pad, dww, dwb, lng, lnb, n_prompt_tiles,
               tiles_per_stream):
    n_w = D_CONV // CW
    n_tiles = n_prompt_tiles + 1
    n_states = (n_tiles + tiles_per_stream - 1) // tiles_per_stream
    n_dec, hist, _ = cache_pad.shape
    dec_seq = TM // n_dec
    assert hist == HIST and dec_seq % SUBLANES == 0 and dec_seq >= CONV_W - 1
    assert n_dec * (HIST + dec_seq) <= SUBLANES * TM
    tile = pl.BlockSpec((TM, D_MODEL), _tile_map(n_w, n_tiles))
    state_map = lambda i: (jnp.maximum(i - n_w, 0) // tiles_per_stream, 0, 0)
    return pl.pallas_call(
        functools.partial(_conv_kernel, n_w=n_w, n_prompt_tiles=n_prompt_tiles,
                          tiles_per_stream=tiles_per_stream, n_dec=n_dec, dec_seq=dec_seq),
        grid=(n_w + n_tiles,),
        in_specs=[
            tile, _const_spec((1, D_MODEL)),
            _w_col_spec(lead, D_MODEL, CW, n_w, 0),
            _w_col_spec(lead, D_MODEL, CW, n_w, n_w),
            _w_row_spec(lead, CW, D_MODEL, n_w),
            _const_spec(cache_pad.shape), _const_spec(dww.shape), _const_spec((1, D_CONV)),
            _const_spec((1, D_CONV)), _const_spec((1, D_CONV)),
        ],
        out_specs=[tile, pl.BlockSpec((1, TM, D_CONV), state_map)],
        out_shape=[jax.ShapeDtypeStruct((n_tiles * TM, D_MODEL), _F32),
                   jax.ShapeDtypeStruct((n_states, TM, D_CONV), _F32)],
        scratch_shapes=[
            pltpu.VMEM((n_w, D_MODEL, 2 * CW), _BF16),
            pltpu.VMEM((D_CONV, D_MODEL), _BF16),
            pltpu.VMEM((TM, D_MODEL), _BF16),
            pltpu.VMEM((SUBLANES, HIST + TM, D_CONV), _F32),
            pltpu.VMEM((TM, D_CONV), _F32),
            pltpu.VMEM((TM, D_CONV), _BF16),
        ],
        compiler_params=pltpu.CompilerParams(
            dimension_semantics=("arbitrary",), vmem_limit_bytes=VMEM_LIMIT),
        name="convmod",
    )(x, g, w_in, w_in, w_out, cache_pad, dww, dwb, lng, lnb)


def kernel(x_prompt, x_sample, cache_conv, norm_g, ffn_w_in, ffn_w_out, a_w_in, a_v_ln_g, a_v_ln_b,
           a_w_s, a_b_s, a_w_out, b_w_in, b_dw_w, b_dw_b, b_ln_g, b_ln_b, b_w_out, final_norm_g):
    batch, seq, _ = x_prompt.shape
    n_dec, dec_seq, _ = x_sample.shape
    depth = norm_g.shape[0]
    assert seq % TM == 0 and n_dec * dec_seq == TM and dec_seq == RB and CHUNK_A % dec_seq == 0
    tiles_per_stream = seq // TM
    n_prompt_tiles = batch * tiles_per_stream

    xs = [x_prompt.reshape(batch * seq, D_MODEL), x_sample.reshape(n_dec * dec_seq, D_MODEL)]
    a_states, conv_states = [], []
    for i in range(depth):
        j = i // 2
        xs = [_ffn_call(xs, norm_g[i, 0][None], ffn_w_in, ffn_w_out, (i, 0), n_prompt_tiles)]
        if i % 2 == 0:
            rep = CHUNK_A // dec_seq
            ws2 = jnp.stack([a_w_s[j], jnp.tile(a_w_s[j][:, :dec_seq, :dec_seq], (1, rep, rep))])
            bs_p = jnp.broadcast_to(a_b_s[j][:, :, None], (N_GROUPS_A, CHUNK_A, D_GROUP_A))
            bs_s = jnp.broadcast_to(jnp.tile(a_b_s[j][:, :dec_seq], (1, rep))[:, :, None],
                                    (N_GROUPS_A, CHUNK_A, D_GROUP_A))
            x, vp, vs = _gmlp_call(xs[0], norm_g[i, 1][None], a_w_in, a_w_out, (j,), a_v_ln_g[j][None],
                                   a_v_ln_b[j][None], ws2, jnp.stack([bs_p, bs_s]),
                                   n_prompt_tiles, tiles_per_stream)
            a_states.append((vp, vs.reshape(n_dec, dec_seq, D_A)))
        else:
            cache_pad = jnp.pad(cache_conv[j], ((0, 0), (CONV_OFF, 0), (0, 0)))
            dww = jnp.pad(b_dw_w[j], ((0, HIST - CONV_W), (0, 0)))
            x, gst = _conv_call(xs[0], norm_g[i, 1][None], b_w_in, b_w_out, (j,), cache_pad, dww,
                                b_dw_b[j][None], b_ln_g[j][None], b_ln_b[j][None],
                                n_prompt_tiles, tiles_per_stream)
            conv_states.append(gst)
        last = i == depth - 1
        res = _ffn_call([x], norm_g[i, 2][None], ffn_w_in, ffn_w_out, (i, 1), n_prompt_tiles,
                        final_norm_g[None] if last else None)
        xs = res if last else [res]

    y_prompt = xs[0].reshape(batch, seq, D_MODEL)
    y_sample = xs[1].reshape(n_dec, dec_seq, D_MODEL)
    gst = jnp.stack(conv_states)
    v_prompt = jnp.stack([vp for vp, _ in a_states])
    v_sample = jnp.stack([vs for _, vs in a_states])
    g_prompt = gst[:, :batch, TM - (CONV_W - 1):, :]
    g_sample = gst[:, batch].reshape(-1, n_dec, dec_seq, D_CONV)[:, :, dec_seq - (CONV_W - 1):, :]
    return (y_prompt, y_sample, v_prompt, v_sample, g_prompt, g_sample)
```

```python
import functools

import jax
import jax.numpy as jnp
from jax import lax
from jax.experimental import pallas as pl
from jax.experimental.pallas import tpu as pltpu

D_MODEL = 1024
D_FF = 2816
D_A = 2 * D_MODEL
CHUNK_A = 128
N_GROUPS_A = 8
D_GROUP_A = D_A // N_GROUPS_A
D_CONV = D_MODEL
CONV_W = 31
EPS = 1e-6

TM = 512
TF = 1024
RB = 32
LN_RB = 16
CONV_RB = 64
CW = 256
LANES = 128
SUBLANES = 8
HIST = 32
CONV_OFF = HIST - (CONV_W - 1)
VMEM_LIMIT = 56 * 1024 * 1024
FFN_VMEM_LIMIT = 58 * 1024 * 1024

_F32 = jnp.float32
_BF16 = jnp.bfloat16


def _const_spec(shape):
    nd = len(shape)
    return pl.BlockSpec(shape, lambda i: (0,) * nd, pipeline_mode=pl.Buffered(1))


def _blocks(start, n_rows, rb):
    return [slice(start + r * rb, start + (r + 1) * rb) for r in range(n_rows // rb)]


def _rms_rows(x, g_ref):
    ms = jnp.mean(x * x, axis=-1, keepdims=True)
    return x * lax.rsqrt(ms + EPS) * g_ref[...]


def _layer_norm_rows(x, g_ref, b_ref):
    mu = jnp.mean(x, axis=-1, keepdims=True)
    xc = x - mu
    var = jnp.mean(xc * xc, axis=-1, keepdims=True)
    return xc * lax.rsqrt(var + EPS) * g_ref[...] + b_ref[...]


def _tile_map(n_w, n_clip):
    return lambda i: (jnp.clip(i - n_w, 0, n_clip - 1), 0)


def _w_col_spec(lead, rows, cols, n_w, offset):
    none = (None,) * len(lead)
    return pl.BlockSpec(none + (rows, cols), lambda i: lead + (0, offset + jnp.minimum(i, n_w - 1)))


def _w_row_spec(lead, rows, cols, n_w):
    none = (None,) * len(lead)
    return pl.BlockSpec(none + (rows, cols), lambda i: lead + (jnp.minimum(i, n_w - 1), 0))


def _ffn_kernel(*refs, n_w, n_big, first, final):
    refs = list(refs)
    big_in = refs.pop(0)
    small_in = refs.pop(0) if first else big_in
    g_ref, wa_ref, wb_ref, wo_ref = (refs.pop(0) for _ in range(4))
    gf_ref = refs.pop(0) if final else None
    big_out = refs.pop(0)
    small_out = refs.pop(0) if final else big_out
    win_s, wout_s, h_ref, act_ref = refs

    i = pl.program_id(0)

    def rms_rows(src_ref, rows):
        h_ref[rows, :] = _rms_rows(src_ref[rows, :], g_ref).astype(_BF16)

    def final_norm(dst_ref, rs):
        for rows in _blocks(rs.start, rs.stop - rs.start, RB):
            dst_ref[rows, :] = _rms_rows(dst_ref[rows, :], gf_ref)

    def half(src_ref, dst_ref, rs):
        for rows in _blocks(rs.start, rs.stop - rs.start, RB):
            rms_rows(src_ref, rows)
        for c in range(n_w):
            ab = jnp.dot(h_ref[rs, :], win_s[c], preferred_element_type=_F32)
            a = ab[:, :CW]
            b = ab[:, CW:]
            act_ref[rs, c * CW:(c + 1) * CW] = (a * jax.nn.sigmoid(a) * b).astype(_BF16)
        y = jnp.dot(act_ref[rs, :], wout_s[...], preferred_element_type=_F32)
        dst_ref[rs, :] = src_ref[rs, :] + 0.5 * y
        if final:
            final_norm(dst_ref, rs)

    halves = [slice(hh * TM, (hh + 1) * TM) for hh in range(TF // TM)]

    @pl.when(i < n_w)
    def _weights_and_tile0():
        win_s[i, :, :CW] = wa_ref[...].astype(_BF16)
        win_s[i, :, CW:] = wb_ref[...].astype(_BF16)
        w_rows = pl.ds(pl.multiple_of(i * CW, CW), CW)
        wout_s[w_rows, :] = wo_ref[...].astype(_BF16)

        @pl.when(i == 0)
        def _start_tile0():
            for rows in _blocks(0, TF, RB):
                rms_rows(big_in, rows)
                big_out[rows, :] = big_in[rows, :]

        for rs in halves:
            ab = jnp.dot(h_ref[rs, :], win_s[i], preferred_element_type=_F32)
            a = ab[:, :CW]
            b = ab[:, CW:]
            act = (a * jax.nn.sigmoid(a) * b).astype(_BF16)
            big_out[rs, :] += 0.5 * jnp.dot(act, wout_s[w_rows, :], preferred_element_type=_F32)

        if final:
            @pl.when(i == n_w - 1)
            def _finish_tile0():
                for rs in halves:
                    final_norm(big_out, rs)

    @pl.when(jnp.logical_and(i >= n_w, i < n_w + n_big - 1))
    def _prompt_tile():
        for rs in halves:
            half(big_in, big_out, rs)

    @pl.when(i == n_w + n_big - 1)
    def _sample_tile():
        half(small_in, small_out, slice(0, TM))


def _ffn_call(xs, g, w_in, w_out, lead, n_prompt_tiles, gf=None):
    first = len(xs) == 2
    final = gf is not None
    n_w = D_FF // CW
    n_big = n_prompt_tiles * TM // TF
    assert n_prompt_tiles * TM % TF == 0
    n_tok = (n_prompt_tiles + 1) * TM

    def big_map(n_clip):
        return lambda i: (jnp.clip(i - (n_w - 1), 0, n_clip - 1), 0)

    big_all = pl.BlockSpec((TF, D_MODEL), big_map(n_big + 1))
    big_prompt = pl.BlockSpec((TF, D_MODEL), big_map(n_big))
    small = pl.BlockSpec((TM, D_MODEL), lambda i: (0, 0))

    in_specs = [big_prompt, small] if first else [big_all]
    in_specs += [
        _const_spec((1, D_MODEL)),
        _w_col_spec(lead, D_MODEL, CW, n_w, 0),
        _w_col_spec(lead, D_MODEL, CW, n_w, n_w),
        _w_row_spec(lead, CW, D_MODEL, n_w),
    ]
    args = list(xs) + [g, w_in, w_in, w_out]
    if final:
        in_specs.append(_const_spec((1, D_MODEL)))
        args.append(gf)
        out_specs = [big_prompt, small]
        out_shape = [jax.ShapeDtypeStruct((n_prompt_tiles * TM, D_MODEL), _F32),
                     jax.ShapeDtypeStruct((TM, D_MODEL), _F32)]
    else:
        out_specs = big_all
        out_shape = jax.ShapeDtypeStruct((n_tok, D_MODEL), _F32)
    return pl.pallas_call(
        functools.partial(_ffn_kernel, n_w=n_w, n_big=n_big, first=first, final=final),
        grid=(n_w + n_big,),
        in_specs=in_specs,
        out_specs=out_specs,
        out_shape=out_shape,
        scratch_shapes=[
            pltpu.VMEM((n_w, D_MODEL, 2 * CW), _BF16),
            pltpu.VMEM((D_FF, D_MODEL), _BF16),
            pltpu.VMEM((TF, D_MODEL), _BF16),
            pltpu.VMEM((TF, D_FF), _BF16),
        ],
        compiler_params=pltpu.CompilerParams(
            dimension_semantics=("arbitrary",), vmem_limit_bytes=FFN_VMEM_LIMIT),
        name="ffn_first" if first else ("ffn_final" if final else "ffn"),
    )(*args)


def _gmlp_kernel(x_ref, g_ref, wi_ref, wo_ref, vg_ref, vb_ref, ws_ref, bs_ref,
                 o_ref, vp_ref, vs_ref, win_s, wout_s, h_ref, uv_ref, vnb_ref, wsb_ref, y_ref,
                 *, n_w, n_prompt_tiles):
    i = pl.program_id(0)
    t = i - n_w
    wo_rows = D_A // n_w
    ncol = 2 * D_A // n_w

    @pl.when(i < n_w)
    def _load_weights():
        win_s[i] = wi_ref[...].astype(_BF16)
        wout_s[pl.ds(pl.multiple_of(i * wo_rows, wo_rows), wo_rows), :] = wo_ref[...].astype(_BF16)

    @pl.when(i >= n_w)
    def _tile():
        row = lax.broadcasted_iota(jnp.int32, (CHUNK_A, CHUNK_A), 0)
        col = lax.broadcasted_iota(jnp.int32, (CHUNK_A, CHUNK_A), 1)
        same_stream = jnp.logical_or(t < n_prompt_tiles, (row // RB) == (col // RB))
        keep = jnp.logical_and(col <= row, same_stream)
        for gi in range(N_GROUPS_A):
            wsb_ref[gi] = jnp.where(keep, ws_ref[0, gi], 0.0).astype(_BF16)

        for rows in _blocks(0, TM, RB):
            h_ref[rows, :] = _rms_rows(x_ref[rows, :], g_ref).astype(_BF16)

        n_half = n_w // 2
        for c in list(range(n_half, n_w)) + list(range(n_half)):
            tt = jnp.dot(h_ref[...], win_s[c], preferred_element_type=_F32)
            uv_ref[:, c * ncol:(c + 1) * ncol] = 0.5 * tt * (1.0 + lax.erf(tt * (2.0 ** -0.5)))
            if c == n_w - 1:
                for rows in _blocks(0, TM, LN_RB):
                    vn = _layer_norm_rows(uv_ref[rows, D_A:], vg_ref, vb_ref)
                    vs_ref[rows, :] = vn
                    vnb_ref[rows, :] = vn.astype(_BF16)

        for rows in _blocks(0, TM, CHUNK_A):
            for gi in range(N_GROUPS_A):
                cols = slice(gi * D_GROUP_A, (gi + 1) * D_GROUP_A)
                z = jnp.dot(wsb_ref[gi], vnb_ref[rows, cols], preferred_element_type=_F32)
                z = z + bs_ref[0, gi]
                y_ref[rows, cols] = (uv_ref[rows, cols] * z).astype(_BF16)

        out = jnp.dot(y_ref[...], wout_s[...], preferred_element_type=_F32)
        o_ref[...] = x_ref[...] + out

        @pl.when(t < n_prompt_tiles)
        def _prompt_state():
            vp_ref[0] = vs_ref[TM - CHUNK_A:TM, :]


def _gmlp_call(x, g, w_in, w_out, lead, vg, vb, ws2, bs2, n_prompt_tiles, tiles_per_stream):
    n_w = 8
    n_tiles = n_prompt_tiles + 1
    n_streams = n_prompt_tiles // tiles_per_stream
    tile = pl.BlockSpec((TM, D_MODEL), _tile_map(n_w, n_tiles))
    group_map = lambda i: (jnp.maximum(i - n_w, 0) // n_prompt_tiles, 0, 0, 0)
    stream_map = lambda i: (jnp.clip((i - n_w) // tiles_per_stream, 0, n_streams - 1), 0, 0)
    return pl.pallas_call(
        functools.partial(_gmlp_kernel, n_w=n_w, n_prompt_tiles=n_prompt_tiles),
        grid=(n_w + n_tiles,),
        in_specs=[
            tile, _const_spec((1, D_MODEL)),
            _w_col_spec(lead, D_MODEL, 2 * D_A // n_w, n_w, 0),
            _w_row_spec(lead, D_A // n_w, D_MODEL, n_w),
            _const_spec((1, D_A)), _const_spec((1, D_A)),
            pl.BlockSpec((1, N_GROUPS_A, CHUNK_A, CHUNK_A), group_map),
            pl.BlockSpec((1, N_GROUPS_A, CHUNK_A, D_GROUP_A), group_map),
        ],
        out_specs=[tile, pl.BlockSpec((1, CHUNK_A, D_A), stream_map),
                   pl.BlockSpec((TM, D_A), lambda i: (0, 0))],
        out_shape=[jax.ShapeDtypeStruct((n_tiles * TM, D_MODEL), _F32),
                   jax.ShapeDtypeStruct((n_streams, CHUNK_A, D_A), _F32),
                   jax.ShapeDtypeStruct((TM, D_A), _F32)],
        scratch_shapes=[
            pltpu.VMEM((n_w, D_MODEL, 2 * D_A // n_w), _BF16),
            pltpu.VMEM((D_A, D_MODEL), _BF16),
            pltpu.VMEM((TM, D_MODEL), _BF16),
            pltpu.VMEM((TM, 2 * D_A), _F32),
            pltpu.VMEM((TM, D_A), _BF16),
            pltpu.VMEM((N_GROUPS_A, CHUNK_A, CHUNK_A), _BF16),
            pltpu.VMEM((TM, D_A), _BF16),
        ],
        compiler_params=pltpu.CompilerParams(
            dimension_semantics=("arbitrary",), vmem_limit_bytes=VMEM_LIMIT),
        name="gmlp",
    )(x, g, w_in, w_out, vg, vb, ws2, bs2)


def _conv_kernel(x_ref, g_ref, wa_ref, wb_ref, wo_ref, cache_ref, dww_ref, dwb_ref, lng_ref, lnb_ref,
                 o_ref, glu_ref, win_s, wout_s, h_ref, xp_ref, c_ref, cn_ref,
                 *, n_w, n_prompt_tiles, tiles_per_stream, n_dec, dec_seq):
    i = pl.program_id(0)
    t = i - n_w

    @pl.when(i < n_w)
    def _load_weights():
        win_s[i, :, :CW] = wa_ref[...].astype(_BF16)
        win_s[i, :, CW:] = wb_ref[...].astype(_BF16)
        wout_s[pl.ds(pl.multiple_of(i * CW, CW), CW), :] = wo_ref[...].astype(_BF16)

    def conv_block(src, dst, rb):
        for cb in range(D_CONV // LANES):
            cols = slice(cb * LANES, (cb + 1) * LANES)
            acc = jnp.broadcast_to(dwb_ref[:, cols], (rb, LANES))
            for k in range(CONV_W):
                o = k + CONV_OFF
                r, q = o % SUBLANES, o // SUBLANES
                acc = acc + xp_ref[r, pl.ds(src + q * SUBLANES, rb), cols] * dww_ref[k:k + 1, cols]
            c_ref[pl.ds(dst, rb), cols] = acc

    def conv_segment(copy, src, dst, rb):
        n_win = rb + HIST
        for cb in range(D_CONV // LANES):
            cols = slice(cb * LANES, (cb + 1) * LANES)
            win = xp_ref[copy, pl.ds(src, n_win), cols]
            acc = jnp.broadcast_to(dwb_ref[:, cols], (rb, LANES))
            for r in range(SUBLANES):
                offs = [o for o in range(r, HIST + 1, SUBLANES) if 0 <= o - CONV_OFF < CONV_W]
                shifted = pltpu.roll(win, n_win - r, 0) if r else win
                for o in offs:
                    k = o - CONV_OFF
                    acc = acc + shifted[o - r:o - r + rb, :] * dww_ref[k:k + 1, cols]
            c_ref[pl.ds(dst, rb), cols] = acc

    def rms_rows(rows):
        h_ref[rows, :] = _rms_rows(x_ref[rows, :], g_ref).astype(_BF16)

    def in_proj(rs, xp_row0):
        for c in range(D_CONV // CW):
            ag = jnp.dot(h_ref[rs, :], win_s[c], preferred_element_type=_F32)
            glu = ag[:, :CW] * jax.nn.sigmoid(ag[:, CW:])
            glu_ref[0, rs, c * CW:(c + 1) * CW] = glu
            if xp_row0 is not None:
                for r in range(SUBLANES):
                    xp_ref[r, xp_row0 - r:xp_row0 - r + rs.stop - rs.start, c * CW:(c + 1) * CW] = glu

    def ln_rows(rows):
        cn = _layer_norm_rows(c_ref[rows, :], lng_ref, lnb_ref)
        cn_ref[rows, :] = (cn * jax.nn.sigmoid(cn)).astype(_BF16)

    def out_proj(rs):
        out = jnp.dot(cn_ref[rs, :], wout_s[...], preferred_element_type=_F32)
        o_ref[rs, :] = x_ref[rs, :] + out

    @pl.when(jnp.logical_and(i >= n_w, t < n_prompt_tiles))
    def _prompt_tile():
        @pl.when(t % tiles_per_stream == 0)
        def _new_stream():
            for r in range(SUBLANES):
                xp_ref[r, 0:HIST, :] = jnp.zeros((HIST, D_CONV), _F32)
                xp_ref[r, TM:TM + HIST, :] = jnp.zeros((HIST, D_CONV), _F32)

        tile = slice(0, TM)
        for rows in _blocks(0, TM, RB):
            rms_rows(rows)
        in_proj(tile, HIST)
        for rows in _blocks(0, TM, CONV_RB):
            conv_block(rows.start, rows.start, CONV_RB)
        for rows in _blocks(0, TM, RB):
            ln_rows(rows)
        out_proj(tile)
        for r in range(SUBLANES):
            xp_ref[r, 0:HIST, :] = xp_ref[r, TM:TM + HIST, :]

    @pl.when(t >= n_prompt_tiles)
    def _sample_tile():
        for rows in _blocks(0, TM, RB):
            rms_rows(rows)
        in_proj(slice(0, TM), None)
        seg = HIST + dec_seq
        per_copy = TM // seg
        for s in range(n_dec):
            row0 = (s % per_copy) * seg
            xp_ref[s // per_copy, row0:row0 + HIST, :] = cache_ref[s]
            xp_ref[s // per_copy, row0 + HIST:row0 + seg, :] = (
                glu_ref[0, s * dec_seq:(s + 1) * dec_seq, :])

        def seg_body(s, carry):
            conv_segment(s // per_copy, pl.multiple_of((s % per_copy) * seg, seg),
                         pl.multiple_of(s * dec_seq, dec_seq), dec_seq)
            return carry
        lax.fori_loop(0, n_dec, seg_body, 0)
        for rows in _blocks(0, TM, RB):
            ln_rows(rows)
        out_proj(slice(0, TM))


def _conv_call(x, g, w_in, w_out, lead, cache_pad, dww, dwb, lng, lnb, n_prompt_tiles,
               tiles_per_stream):
    n_w = D_CONV // CW
    n_tiles = n_prompt_tiles + 1
    n_states = (n_tiles + tiles_per_stream - 1) // tiles_per_stream
    n_dec, hist, _ = cache_pad.shape
    dec_seq = TM // n_dec
    assert hist == HIST and dec_seq % SUBLANES == 0 and dec_seq >= CONV_W - 1
    assert n_dec * (HIST + dec_seq) <= SUBLANES * TM
    tile = pl.BlockSpec((TM, D_MODEL), _tile_map(n_w, n_tiles))
    state_map = lambda i: (jnp.maximum(i - n_w, 0) // tiles_per_stream, 0, 0)
    return pl.pallas_call(
        functools.partial(_conv_kernel, n_w=n_w, n_prompt_tiles=n_prompt_tiles,
                          tiles_per_stream=tiles_per_stream, n_dec=n_dec, dec_seq=dec_seq),
        grid=(n_w + n_tiles,),
        in_specs=[
            tile, _const_spec((1, D_MODEL)),
            _w_col_spec(lead, D_MODEL, CW, n_w, 0),
            _w_col_spec(lead, D_MODEL, CW, n_w, n_w),
            _w_row_spec(lead, CW, D_MODEL, n_w),
            _const_spec(cache_pad.shape), _const_spec(dww.shape), _const_spec((1, D_CONV)),
            _const_spec((1, D_CONV)), _const_spec((1, D_CONV)),
        ],
        out_specs=[tile, pl.BlockSpec((1, TM, D_CONV), state_map)],
        out_shape=[jax.ShapeDtypeStruct((n_tiles * TM, D_MODEL), _F32),
                   jax.ShapeDtypeStruct((n_states, TM, D_CONV), _F32)],
        scratch_shapes=[
            pltpu.VMEM((n_w, D_MODEL, 2 * CW), _BF16),
            pltpu.VMEM((D_CONV, D_MODEL), _BF16),
            pltpu.VMEM((TM, D_MODEL), _BF16),
            pltpu.VMEM((SUBLANES, HIST + TM, D_CONV), _F32),
            pltpu.VMEM((TM, D_CONV), _F32),
            pltpu.VMEM((TM, D_CONV), _BF16),
        ],
        compiler_params=pltpu.CompilerParams(
            dimension_semantics=("arbitrary",), vmem_limit_bytes=VMEM_LIMIT),
        name="convmod",
    )(x, g, w_in, w_in, w_out, cache_pad, dww, dwb, lng, lnb)


def kernel(x_prompt, x_sample, cache_conv, norm_g, ffn_w_in, ffn_w_out, a_w_in, a_v_ln_g, a_v_ln_b,
           a_w_s, a_b_s, a_w_out, b_w_in, b_dw_w, b_dw_b, b_ln_g, b_ln_b, b_w_out, final_norm_g):
    batch, seq, _ = x_prompt.shape
    n_dec, dec_seq, _ = x_sample.shape
    depth = norm_g.shape[0]
    assert seq % TM == 0 and n_dec * dec_seq == TM and dec_seq == RB and CHUNK_A % dec_seq == 0
    tiles_per_stream = seq // TM
    n_prompt_tiles = batch * tiles_per_stream

    xs = [x_prompt.reshape(batch * seq, D_MODEL), x_sample.reshape(n_dec * dec_seq, D_MODEL)]
    a_states, conv_states = [], []
    for i in range(depth):
        j = i // 2
        xs = [_ffn_call(xs, norm_g[i, 0][None], ffn_w_in, ffn_w_out, (i, 0), n_prompt_tiles)]
        if i % 2 == 0:
            rep = CHUNK_A // dec_seq
            ws2 = jnp.stack([a_w_s[j], jnp.tile(a_w_s[j][:, :dec_seq, :dec_seq], (1, rep, rep))])
            bs_p = jnp.broadcast_to(a_b_s[j][:, :, None], (N_GROUPS_A, CHUNK_A, D_GROUP_A))
            bs_s = jnp.broadcast_to(jnp.tile(a_b_s[j][:, :dec_seq], (1, rep))[:, :, None],
                                    (N_GROUPS_A, CHUNK_A, D_GROUP_A))
            x, vp, vs = _gmlp_call(xs[0], norm_g[i, 1][None], a_w_in, a_w_out, (j,), a_v_ln_g[j][None],
                                   a_v_ln_b[j][None], ws2, jnp.stack([bs_p, bs_s]),
                                   n_prompt_tiles, tiles_per_stream)
            a_states.append((vp, vs.reshape(n_dec, dec_seq, D_A)))
        else:
            cache_pad = jnp.pad(cache_conv[j], ((0, 0), (CONV_OFF, 0), (0, 0)))
            dww = jnp.pad(b_dw_w[j], ((0, HIST - CONV_W), (0, 0)))
            x, gst = _conv_call(xs[0], norm_g[i, 1][None], b_w_in, b_w_out, (j,), cache_pad, dww,
                                b_dw_b[j][None], b_ln_g[j][None], b_ln_b[j][None],
                                n_prompt_tiles, tiles_per_stream)
            conv_states.append(gst)
        last = i == depth - 1
        res = _ffn_call([x], norm_g[i, 2][None], ffn_w_in, ffn_w_out, (i, 1), n_prompt_tiles,
                        final_norm_g[None] if last else None)
        xs = res if last else [res]

    y_prompt = xs[0].reshape(batch, seq, D_MODEL)
    y_sample = xs[1].reshape(n_dec, dec_seq, D_MODEL)
    gst = jnp.stack(conv_states)
    v_prompt = jnp.stack([vp for vp, _ in a_states])
    v_sample = jnp.stack([vs for _, vs in a_states])
    g_prompt = gst[:, :batch, TM - (CONV_W - 1):, :]
    g_sample = gst[:, batch].reshape(-1, n_dec, dec_seq, D_CONV)[:, :, dec_seq - (CONV_W - 1):, :]
    return (y_prompt, y_sample, v_prompt, v_sample, g_prompt, g_sample)
```
